```python
import jax, jax.numpy as jnp
from jax import lax
import numpy as np

D_MODEL = 2048
BATCH = 4
SEQ = 4096
DEPTH = 2

D_POOL = D_MODEL // 2
POOL_WINDOWS = (2, 4, 8, 16)
N_POOL_GROUPS = len(POOL_WINDOWS)
POOL_GROUP = D_POOL // N_POOL_GROUPS
D_SGU = D_MODEL // 2
SGU_CHUNK = 128
N_SGU_GROUPS = 8
SGU_GROUP = D_SGU // N_SGU_GROUPS
D_CONV = D_MODEL // 2
CONV_WIDTH = 3
N_BRANCH = 3
D_FF = 5632
EPS = 1e-6

OFF_POOL = 0
OFF_SGU = OFF_POOL + D_POOL
OFF_CONV = OFF_SGU + 2 * D_SGU
OFF_GATE = OFF_CONV + 3 * D_CONV
D_IN = OFF_GATE + N_BRANCH * D_MODEL

kernel_name = "hybrid_pool_sgu_shortconv_encoder"


def rmsnorm(x, g):
    xf = x.astype(jnp.float32)
    y = xf * lax.rsqrt(jnp.mean(xf * xf, axis=-1, keepdims=True) + EPS)
    return (y * g.astype(jnp.float32)).astype(x.dtype)


def dwconv3_centred(x, w):
    xp = jnp.pad(x, ((0, 0), (1, 1), (0, 0)))
    return xp[:, :-2] * w[0] + xp[:, 1:-1] * w[1] + xp[:, 2:] * w[2]


def multiscale_pool_mixer(a, w_pool, pool_scale):
    bsz, L, _ = a.shape
    af = a.astype(jnp.float32)
    csum = jnp.concatenate([jnp.zeros((bsz, 1, D_POOL), jnp.float32), jnp.cumsum(af, axis=1)], axis=1)
    t = jnp.arange(L, dtype=jnp.int32)
    outs = []
    for gi, w in enumerate(POOL_WINDOWS):
        lo = jnp.clip(t - w // 2, 0, L - 1)
        hi = jnp.clip(t + (w - w // 2 - 1), 0, L - 1)
        cg = csum[..., gi * POOL_GROUP:(gi + 1) * POOL_GROUP]
        s = jnp.take(cg, hi + 1, axis=1) - jnp.take(cg, lo, axis=1)
        cnt = (hi - lo + 1).astype(jnp.float32)[None, :, None]
        outs.append(s / cnt)
    pooled = jnp.concatenate(outs, axis=-1) - af
    pooled = pooled.astype(a.dtype).reshape(bsz, L, N_POOL_GROUPS, POOL_GROUP)
    y = jnp.einsum('blgc,gcd->blgd', pooled, w_pool).reshape(bsz, L, D_POOL)
    return y * pool_scale


def spatial_gating_mixer(uv, sgu_norm_g, sgu_w, sgu_b):
    bsz, L, _ = uv.shape
    uv = jax.nn.gelu(uv)
    u, v = uv[..., :D_SGU], uv[..., D_SGU:]
    v = rmsnorm(v, sgu_norm_g)
    v = v.reshape(bsz, L // SGU_CHUNK, SGU_CHUNK, N_SGU_GROUPS, SGU_GROUP)
    z = jnp.einsum('gpq,bnqgc->bnpgc', sgu_w, v) + sgu_b[None, None, :, :, None]
    return u * z.reshape(bsz, L, D_SGU)


def short_conv_mixer(xbc, conv_w):
    xc = xbc[..., :D_CONV]
    bg = xbc[..., D_CONV:2 * D_CONV]
    cg = xbc[..., 2 * D_CONV:]
    return bg * dwconv3_centred(cg * xc, conv_w)


def conv_glu_ffn(h, w_up, ffn_conv_w, w_down):
    up = dwconv3_centred(h @ w_up, ffn_conv_w)
    gate, val = up[..., :D_FF], up[..., D_FF:]
    return (jax.nn.silu(gate) * val) @ w_down


def setup_inputs(seed: int = 0) -> dict:
    key = jax.random.key(seed)
    ks = jax.random.split(key, 20)
    f32 = jnp.float32
    nrm = lambda k, shape, s: jax.random.normal(k, shape, f32) * s
    return {
        "x": jax.random.normal(ks[0], (BATCH, SEQ, D_MODEL), f32),
        "norm1_g": 1.0 + nrm(ks[1], (DEPTH, D_MODEL), 0.02),
        "w_in": nrm(ks[2], (DEPTH, D_MODEL, D_IN), D_MODEL ** -0.5),
        "w_pool": nrm(ks[3], (DEPTH, N_POOL_GROUPS, POOL_GROUP, POOL_GROUP), POOL_GROUP ** -0.5),
        "pool_scale": 1.0 + nrm(ks[4], (DEPTH, D_POOL), 0.1),
        "sgu_norm_g": 1.0 + nrm(ks[5], (DEPTH, D_SGU), 0.02),
        "sgu_w": nrm(ks[6], (DEPTH, N_SGU_GROUPS, SGU_CHUNK, SGU_CHUNK), SGU_CHUNK ** -0.5),
        "sgu_b": 1.0 + nrm(ks[7], (DEPTH, SGU_CHUNK, N_SGU_GROUPS), 0.02),
        "conv_w": nrm(ks[8], (DEPTH, CONV_WIDTH, D_CONV), CONV_WIDTH ** -0.5),
        "w_pool_out": nrm(ks[9], (DEPTH, D_POOL, D_MODEL), D_POOL ** -0.5),
        "w_sgu_out": nrm(ks[10], (DEPTH, D_SGU, D_MODEL), D_SGU ** -0.5),
        "w_conv_out": nrm(ks[11], (DEPTH, D_CONV, D_MODEL), D_CONV ** -0.5),
        "w_o": nrm(ks[12], (DEPTH, D_MODEL, D_MODEL), D_MODEL ** -0.5),
        "norm2_g": 1.0 + nrm(ks[13], (DEPTH, D_MODEL), 0.02),
        "w_up": nrm(ks[14], (DEPTH, D_MODEL, 2 * D_FF), D_MODEL ** -0.5),
        "ffn_conv_w": nrm(ks[15], (DEPTH, CONV_WIDTH, 2 * D_FF), CONV_WIDTH ** -0.5),
        "w_down": nrm(ks[16], (DEPTH, D_FF, D_MODEL), D_FF ** -0.5),
        "final_g": 1.0 + nrm(ks[17], (D_MODEL,), 0.02),
    }


def reference(x, norm1_g, w_in, w_pool, pool_scale, sgu_norm_g, sgu_w, sgu_b, conv_w,
              w_pool_out, w_sgu_out, w_conv_out, w_o, norm2_g, w_up, ffn_conv_w, w_down, final_g):
    bsz, L, _ = x.shape
    for l in range(DEPTH):
        h = rmsnorm(x, norm1_g[l])
        proj = h @ w_in[l]
        y_a = multiscale_pool_mixer(proj[..., OFF_POOL:OFF_SGU], w_pool[l], pool_scale[l])
        y_b = spatial_gating_mixer(proj[..., OFF_SGU:OFF_CONV], sgu_norm_g[l], sgu_w[l], sgu_b[l])
        y_c = short_conv_mixer(proj[..., OFF_CONV:OFF_GATE], conv_w[l])
        gates = jax.nn.sigmoid(proj[..., OFF_GATE:].astype(jnp.float32)).astype(x.dtype)
        gates = gates.reshape(bsz, L, N_BRANCH, D_MODEL)
        merged = (gates[..., 0, :] * (y_a @ w_pool_out[l])
                  + gates[..., 1, :] * (y_b @ w_sgu_out[l])
                  + gates[..., 2, :] * (y_c @ w_conv_out[l]))
        x = x + merged @ w_o[l]
        x = x + conv_glu_ffn(rmsnorm(x, norm2_g[l]), w_up[l], ffn_conv_w[l], w_down[l])
    return rmsnorm(x, final_g)
```

```python
import functools

import jax
import jax.numpy as jnp
from jax import lax
from jax.experimental import pallas as pl
from jax.experimental.pallas import tpu as pltpu

D_MODEL = 2048
D_BRANCH = 1024
POOL_WINDOWS = (2, 4, 8, 16)
POOL_GROUP = D_BRANCH // len(POOL_WINDOWS)
SGU_CHUNK = 128
N_SGU_GROUPS = 8
SGU_GROUP = D_BRANCH // N_SGU_GROUPS
D_FF = 5632
EPS = 1e-6
OFF_GATE = 6 * D_BRANCH

HALO = 16
V7X_VMEM_LIMIT_BYTES = 58 * 1024 * 1024

F32 = jnp.float32
BF16 = jnp.bfloat16


def _rmsnorm(xf, g):
    return xf * lax.rsqrt(jnp.mean(xf * xf, axis=-1, keepdims=True) + EPS) * g


def _dot(a, b):
    return jnp.dot(a, b, preferred_element_type=F32)


def _shift_rows(x, k):
    n = x.shape[0]
    return pltpu.roll(x, (-k) % n, axis=0)


def _norm_with_halo(h_scr, xm_ref, xp_ref, xn_ref, g, has_prev, has_next, tm):
    h_scr[0:HALO, :] = jnp.where(has_prev, _rmsnorm(xp_ref[...], g), 0.0).astype(BF16)
    h_scr[HALO:HALO + tm, :] = _rmsnorm(xm_ref[...], g).astype(BF16)
    h_scr[HALO + tm:, :] = jnp.where(has_next, _rmsnorm(xn_ref[...], g), 0.0).astype(BF16)


_MIXER_IN_BLOCKS = (0, 2, 1, 3, 5, 4)


def _mixer_in_kernel(xm_ref, xp_ref, xn_ref, g1_ref, win_ref, wpool_ref, pscale_ref, sgug_ref,
                     sguw_ref, sgub_ref, convw_ref, y_ref, h_scr, cx_scr, vn_scr, z_scr,
                     *, tm, tiles_per_seq, seq_len):
    i = pl.program_id(0)
    s = pl.program_id(1)
    ti = i % tiles_per_seq
    n_ext = tm + 2 * HALO

    @pl.when(s == 0)
    def _pool():
        _norm_with_halo(h_scr, xm_ref, xp_ref, xn_ref, g1_ref[...], ti > 0, ti < tiles_per_seq - 1, tm)
        a = _dot(h_scr[...], win_ref[...])
        pos = ti * tm + lax.broadcasted_iota(jnp.int32, (tm, POOL_GROUP), 0)
        for gi, w in enumerate(POOL_WINDOWS):
            cols = slice(gi * POOL_GROUP, (gi + 1) * POOL_GROUP)
            xg = a[:, cols]
            q, m = xg, 1
            while m < w // 2:
                q = q + _shift_rows(q, m)
                m *= 2
            win_sum = (q + _shift_rows(q, -(w // 2)))[HALO:HALO + tm]
            lo = jnp.maximum(pos - w // 2, 0)
            hi = jnp.minimum(pos + (w - w // 2 - 1), seq_len - 1)
            cnt = (hi - lo + 1).astype(F32)
            pooled = win_sum / cnt - xg[HALO:HALO + tm]
            y = _dot(pooled.astype(BF16), wpool_ref[gi]) * pscale_ref[:, cols]
            y_ref[:, cols] = y.astype(BF16)

    @pl.when(s == 1)
    def _sgu_v():
        gv = jax.nn.gelu(_dot(h_scr[HALO:HALO + tm, :], win_ref[...]))
        vn_scr[...] = _rmsnorm(gv, sgug_ref[...]).astype(BF16)
        n_chunk = tm // SGU_CHUNK
        for g in range(N_SGU_GROUPS):
            cols = slice(g * SGU_GROUP, (g + 1) * SGU_GROUP)
            rhs = jnp.concatenate(
                [vn_scr[c * SGU_CHUNK:(c + 1) * SGU_CHUNK, cols] for c in range(n_chunk)], axis=1)
            o = _dot(sguw_ref[g], rhs)
            for c in range(n_chunk):
                z_scr[c * SGU_CHUNK:(c + 1) * SGU_CHUNK, cols] = (
                    o[:, c * SGU_GROUP:(c + 1) * SGU_GROUP] + sgub_ref[g])

    @pl.when(s == 2)
    def _sgu_u():
        u = jax.nn.gelu(_dot(h_scr[HALO:HALO + tm, :], win_ref[...]))
        y_ref[:, D_BRANCH:2 * D_BRANCH] = (u * z_scr[...]).astype(BF16)

    @pl.when(s == 3)
    def _conv_x():
        cx_scr[...] = _dot(h_scr[...], win_ref[...])

    @pl.when(s == 4)
    def _conv_c():
        cx_scr[...] = cx_scr[...] * _dot(h_scr[...], win_ref[...])

    @pl.when(s == 5)
    def _conv_b():
        cx = cx_scr[...]
        conv = (convw_ref[0:1, :] * _shift_rows(cx, -1) + convw_ref[1:2, :] * cx
                + convw_ref[2:3, :] * _shift_rows(cx, 1))[HALO:HALO + tm]
        bg = _dot(h_scr[HALO:HALO + tm, :], win_ref[...])
        y_ref[:, 2 * D_BRANCH:] = (bg * conv).astype(BF16)


def _halo_specs(tm, n_rows):
    blocks_per_tile = tm // HALO
    last = n_rows // HALO - 1
    prev = pl.BlockSpec((HALO, D_MODEL), lambda i, s: (jnp.maximum(i * blocks_per_tile - 1, 0), 0))
    nxt = pl.BlockSpec((HALO, D_MODEL), lambda i, s: (jnp.minimum((i + 1) * blocks_per_tile, last), 0))
    return prev, nxt


def _mixer_in(x2d, layer, p, *, tm, seq_len):
    n_rows = x2d.shape[0]
    n_steps = len(_MIXER_IN_BLOCKS)

    def win_index(i, s):
        blk = jnp.int32(_MIXER_IN_BLOCKS[-1])
        for step in range(n_steps - 2, -1, -1):
            blk = jnp.where(s == step, _MIXER_IN_BLOCKS[step], blk)
        return (layer, 0, blk)

    prev_spec, next_spec = _halo_specs(tm, n_rows)
    const = lambda *shape: pl.BlockSpec((None,) + shape, lambda i, s: (layer,) + (0,) * len(shape))
    kernel = functools.partial(_mixer_in_kernel, tm=tm, tiles_per_seq=seq_len // tm, seq_len=seq_len)
    return pl.pallas_call(
        kernel,
        grid=(n_rows // tm, n_steps),
        in_specs=[
            pl.BlockSpec((tm, D_MODEL), lambda i, s: (i, 0)),
            prev_spec, next_spec,
            const(1, D_MODEL),
            pl.BlockSpec((None, D_MODEL, D_BRANCH), win_index),
            const(len(POOL_WINDOWS), POOL_GROUP, POOL_GROUP),
            const(1, D_BRANCH),
            const(1, D_BRANCH),
            const(N_SGU_GROUPS, SGU_CHUNK, SGU_CHUNK),
            const(N_SGU_GROUPS, SGU_CHUNK, SGU_GROUP),
            const(3, D_BRANCH),
        ],
        out_specs=pl.BlockSpec((tm, 3 * D_BRANCH), lambda i, s: (i, 0)),
        out_shape=jax.ShapeDtypeStruct((n_rows, 3 * D_BRANCH), BF16),
        scratch_shapes=[
            pltpu.VMEM((tm + 2 * HALO, D_MODEL), BF16),
            pltpu.VMEM((tm + 2 * HALO, D_BRANCH), F32),
            pltpu.VMEM((tm, D_BRANCH), BF16),
            pltpu.VMEM((tm, D_BRANCH), F32),
        ],
        compiler_params=pltpu.CompilerParams(
            dimension_semantics=("parallel", "arbitrary"), vmem_limit_bytes=V7X_VMEM_LIMIT_BYTES),
        name="mixer_in",
    )(x2d, x2d, x2d, p["norm1_g"], p["w_in"], p["w_pool"], p["pool_scale"], p["sgu_norm_g"],
      p["sgu_w"], p["sgu_b"], p["conv_w"])


def _mixer_out_kernel(x_ref, xres_ref, y_ref, g1_ref, wga_ref, wgb_ref, wgc_ref, wa_ref, wb_ref, wc_ref,
                      wo_ref, out_ref, h_scr, m_scr, *, n_merge, bn_merge):
    s = pl.program_id(1)

    @pl.when(s == 0)
    def _norm():
        h_scr[...] = _rmsnorm(x_ref[...], g1_ref[...]).astype(BF16)

    @pl.when(s < n_merge)
    def _merge():
        h = h_scr[...]
        merged = (jax.nn.sigmoid(_dot(h, wga_ref[...])) * _dot(y_ref[:, 0:D_BRANCH], wa_ref[...])
                  + jax.nn.sigmoid(_dot(h, wgb_ref[...])) * _dot(y_ref[:, D_BRANCH:2 * D_BRANCH], wb_ref[...])
                  + jax.nn.sigmoid(_dot(h, wgc_ref[...])) * _dot(y_ref[:, 2 * D_BRANCH:], wc_ref[...]))
        m_scr[s] = merged.astype(BF16)

    @pl.when(s >= n_merge)
    def _out():
        acc = xres_ref[...]
        for k in range(n_merge):
            acc = acc + _dot(m_scr[k], wo_ref[k * bn_merge:(k + 1) * bn_merge, :])
        out_ref[...] = acc


def _mixer_out(x2d, y, layer, p, *, tm, bn_merge, bn_out):
    n_rows = x2d.shape[0]
    n_merge = D_MODEL // bn_merge
    n_out = D_MODEL // bn_out

    def merge_blk(s):
        return jnp.minimum(s, n_merge - 1)

    def out_blk(s):
        return jnp.maximum(s - n_merge, 0)

    def gate_spec(branch):
        first = (OFF_GATE + branch * D_MODEL) // bn_merge
        return pl.BlockSpec((None, D_MODEL, bn_merge), lambda i, s: (layer, 0, first + merge_blk(s)))

    branch_spec = pl.BlockSpec((None, D_BRANCH, bn_merge), lambda i, s: (layer, 0, merge_blk(s)))
    kernel = functools.partial(_mixer_out_kernel, n_merge=n_merge, bn_merge=bn_merge)
    return pl.pallas_call(
        kernel,
        grid=(n_rows // tm, n_merge + n_out),
        in_specs=[
            pl.BlockSpec((tm, D_MODEL), lambda i, s: (i, 0)),
            pl.BlockSpec((tm, bn_out), lambda i, s: (i, out_blk(s))),
            pl.BlockSpec((tm, 3 * D_BRANCH), lambda i, s: (i, 0)),
            pl.BlockSpec((None, 1, D_MODEL), lambda i, s: (layer, 0, 0)),
            gate_spec(0), gate_spec(1), gate_spec(2),
            branch_spec, branch_spec, branch_spec,
            pl.BlockSpec((None, D_MODEL, bn_out), lambda i, s: (layer, 0, out_blk(s))),
        ],
        out_specs=pl.BlockSpec((tm, bn_out), lambda i, s: (i, out_blk(s))),
        out_shape=jax.ShapeDtypeStruct((n_rows, D_MODEL), F32),
        scratch_shapes=[
            pltpu.VMEM((tm, D_MODEL), BF16),
            pltpu.VMEM((n_merge, tm, bn_merge), BF16),
        ],
        compiler_params=pltpu.CompilerParams(
            dimension_semantics=("parallel", "arbitrary"), vmem_limit_bytes=V7X_VMEM_LIMIT_BYTES),
        name="mixer_out",
    )(x2d, x2d, y, p["norm1_g"], p["w_in"], p["w_in"], p["w_in"], p["w_pool_out"], p["w_sgu_out"],
      p["w_conv_out"], p["w_o"])


def _ffn_kernel(xm_ref, xp_ref, xn_ref, g2_ref, wg_ref, wv_ref, cwg_ref, cwv_ref, wd_ref, gf_ref,
                out_ref, h_scr, *, tm, tiles_per_seq, n_steps, final_norm):
    i = pl.program_id(0)
    s = pl.program_id(1)
    ti = i % tiles_per_seq

    @pl.when(s == 0)
    def _init():
        _norm_with_halo(h_scr, xm_ref, xp_ref, xn_ref, g2_ref[...], ti > 0, ti < tiles_per_seq - 1, tm)
        out_ref[...] = xm_ref[...]

    def conv3(up, cw_ref):
        return (cw_ref[0:1, :] * _shift_rows(up, -1) + cw_ref[1:2, :] * up
                + cw_ref[2:3, :] * _shift_rows(up, 1))[HALO:HALO + tm]

    h = h_scr[...]
    gate = conv3(_dot(h, wg_ref[...]), cwg_ref)
    val = conv3(_dot(h, wv_ref[...]), cwv_ref)
    act = (jax.nn.silu(gate) * val).astype(BF16)
    out_ref[...] += _dot(act, wd_ref[...])

    if final_norm:
        @pl.when(s == n_steps - 1)
        def _final():
            out_ref[...] = _rmsnorm(out_ref[...], gf_ref[...])


def _ffn(x2d, layer, p, final_g, *, tm, bn, seq_len, final_norm):
    n_rows = x2d.shape[0]
    n_steps = D_FF // bn
    prev_spec, next_spec = _halo_specs(tm, n_rows)
    kernel = functools.partial(_ffn_kernel, tm=tm, tiles_per_seq=seq_len // tm, n_steps=n_steps,
                               final_norm=final_norm)
    return pl.pallas_call(
        kernel,
        grid=(n_rows // tm, n_steps),
        in_specs=[
            pl.BlockSpec((tm, D_MODEL), lambda i, s: (i, 0)),
            prev_spec, next_spec,
            pl.BlockSpec((None, 1, D_MODEL), lambda i, s: (layer, 0, 0)),
            pl.BlockSpec((None, D_MODEL, bn), lambda i, s: (layer, 0, s)),
            pl.BlockSpec((None, D_MODEL, bn), lambda i, s: (layer, 0, n_steps + s)),
            pl.BlockSpec((None, 3, bn), lambda i, s: (layer, 0, s)),
            pl.BlockSpec((None, 3, bn), lambda i, s: (layer, 0, n_steps + s)),
            pl.BlockSpec((None, bn, D_MODEL), lambda i, s: (layer, s, 0)),
            pl.BlockSpec((1, D_MODEL), lambda i, s: (0, 0)),
        ],
        out_specs=pl.BlockSpec((tm, D_MODEL), lambda i, s: (i, 0)),
        out_shape=jax.ShapeDtypeStruct((n_rows, D_MODEL), F32),
        scratch_shapes=[pltpu.VMEM((tm + 2 * HALO, D_MODEL), BF16)],
        compiler_params=pltpu.CompilerParams(
            dimension_semantics=("parallel", "arbitrary"), vmem_limit_bytes=V7X_VMEM_LIMIT_BYTES),
        name="ffn",
    )(x2d, x2d, x2d, p["norm2_g"], p["w_up"], p["w_up"], p["ffn_conv_w"], p["ffn_conv_w"], p["w_down"],
      final_g)


def kernel(x, norm1_g, w_in, w_pool, pool_scale, sgu_norm_g, sgu_w, sgu_b, conv_w, w_pool_out, w_sgu_out,
           w_conv_out, w_o, norm2_g, w_up, ffn_conv_w, w_down, final_g):
    bsz, seq_len, d_model = x.shape
    depth = w_in.shape[0]
    assert d_model == D_MODEL and w_in.shape[2] == OFF_GATE + 3 * D_MODEL and w_up.shape[2] == 2 * D_FF
    tm = 512
    assert seq_len % tm == 0 and tm % SGU_CHUNK == 0

    p = {
        "norm1_g": norm1_g[:, None, :], "norm2_g": norm2_g[:, None, :],
        "pool_scale": pool_scale[:, None, :], "sgu_norm_g": sgu_norm_g[:, None, :],
        "w_in": w_in.astype(BF16), "w_pool": w_pool.astype(BF16), "sgu_w": sgu_w.astype(BF16),
        "sgu_b": jnp.broadcast_to(jnp.transpose(sgu_b, (0, 2, 1))[..., None],
                                  (depth, N_SGU_GROUPS, SGU_CHUNK, SGU_GROUP)),
        "conv_w": conv_w, "ffn_conv_w": ffn_conv_w,
        "w_pool_out": w_pool_out.astype(BF16), "w_sgu_out": w_sgu_out.astype(BF16),
        "w_conv_out": w_conv_out.astype(BF16), "w_o": w_o.astype(BF16),
        "w_up": w_up.astype(BF16), "w_down": w_down.astype(BF16),
    }
    x2d = x.reshape(bsz * seq_len, d_model)
    for layer in range(depth):
        y = _mixer_in(x2d, layer, p, tm=tm, seq_len=seq_len)
        x2d = _mixer_out(x2d, y, layer, p, tm=tm, bn_merge=512, bn_out=512)
        x2d = _ffn(x2d, layer, p, final_g[None, :], tm=tm, bn=512, seq_len=seq_len,
                   final_norm=(layer == depth - 1))
    return x2d.reshape(bsz, seq_len, d_model)
```

```python
import functools

import jax
import jax.numpy as jnp
from jax import lax
from jax.experimental import pallas as pl
from jax.experimental.pallas import tpu as pltpu

D_MODEL = 2048
D_BRANCH = 1024
POOL_WINDOWS = (2, 4, 8, 16)
POOL_GROUP = D_BRANCH // len(POOL_WINDOWS)
SGU_CHUNK = 128
N_SGU_GROUPS = 8
SGU_GROUP = D_BRANCH // N_SGU_GROUPS
D_FF = 5632
EPS = 1e-6
OFF_GATE = 6 * D_BRANCH

HALO = 16
V7X_VMEM_LIMIT_BYTES = 58 * 1024 * 1024

F32 = jnp.float32
BF16 = jnp.bfloat16


def _rmsnorm(xf, g):
    return xf * lax.rsqrt(jnp.mean(xf * xf, axis=-1, keepdims=True) + EPS) * g


def _dot(a, b):
    return jnp.dot(a, b, preferred_element_type=F32)


def _shift_rows(x, k):
    n = x.shape[0]
    return pltpu.roll(x, (-k) % n, axis=0)


def _norm_with_halo(h_scr, xm_ref, xp_ref, xn_ref, g, has_prev, has_next, tm):
    h_scr[0:HALO, :] = jnp.where(has_prev, _rmsnorm(xp_ref[...], g), 0.0).astype(BF16)
    h_scr[HALO:HALO + tm, :] = _rmsnorm(xm_ref[...], g).astype(BF16)
    h_scr[HALO + tm:, :] = jnp.where(has_next, _rmsnorm(xn_ref[...], g), 0.0).astype(BF16)


_MIXER_IN_BLOCKS = (0, 2, 1, 3, 5, 4)


def _mixer_in_kernel(xm_ref, xp_ref, xn_ref, g1_ref, win_ref, wpool_ref, pscale_ref, sgug_ref,
                     sguw_ref, sgub_ref, convw_ref, y_ref, h_scr, cx_scr, vn_scr, z_scr,
                     *, tm, tiles_per_seq, seq_len):
    i = pl.program_id(0)
    s = pl.program_id(1)
    ti = i % tiles_per_seq
    n_ext = tm + 2 * HALO

    @pl.when(s == 0)
    def _pool():
        _norm_with_halo(h_scr, xm_ref, xp_ref, xn_ref, g1_ref[...], ti > 0, ti < tiles_per_seq - 1, tm)
        a = _dot(h_scr[...], win_ref[...])
        pos = ti * tm + lax.broadcasted_iota(jnp.int32, (tm, POOL_GROUP), 0)
        for gi, w in enumerate(POOL_WINDOWS):
            cols = slice(gi * POOL_GROUP, (gi + 1) * POOL_GROUP)
            xg = a[:, cols]
            q, m = xg, 1
            while m < w // 2:
                q = q + _shift_rows(q, m)
                m *= 2
            win_sum = (q + _shift_rows(q, -(w // 2)))[HALO:HALO + tm]
            lo = jnp.maximum(pos - w // 2, 0)
            hi = jnp.minimum(pos + (w - w // 2 - 1), seq_len - 1)
            cnt = (hi - lo + 1).astype(F32)
            pooled = win_sum / cnt - xg[HALO:HALO + tm]
            y = _dot(pooled.astype(BF16), wpool_ref[gi]) * pscale_ref[:, cols]
            y_ref[:, cols] = y.astype(BF16)

    @pl.when(s == 1)
    def _sgu_v():
        gv = jax.nn.gelu(_dot(h_scr[HALO:HALO + tm, :], win_ref[...]))
        vn_scr[...] = _rmsnorm(gv, sgug_ref[...]).astype(BF16)
        n_chunk = tm // SGU_CHUNK
        for g in range(N_SGU_GROUPS):
            cols = slice(g * SGU_GROUP, (g + 1) * SGU_GROUP)
            rhs = jnp.concatenate(
                [vn_scr[c * SGU_CHUNK:(c + 1) * SGU_CHUNK, cols] for c in range(n_chunk)], axis=1)
            o = _dot(sguw_ref[g], rhs)
            for c in range(n_chunk):
                z_scr[c * SGU_CHUNK:(c + 1) * SGU_CHUNK, cols] = (
                    o[:, c * SGU_GROUP:(c + 1) * SGU_GROUP] + sgub_ref[g])

    @pl.when(s == 2)
    def _sgu_u():
        u = jax.nn.gelu(_dot(h_scr[HALO:HALO + tm, :], win_ref[...]))
        y_ref[:, D_BRANCH:2 * D_BRANCH] = (u * z_scr[...]).astype(BF16)

    @pl.when(s == 3)
    def _conv_x():
        cx_scr[...] = _dot(h_scr[...], win_ref[...])

    @pl.when(s == 4)
    def _conv_c():
        cx_scr[...] = cx_scr[...] * _dot(h_scr[...], win_ref[...])

    @pl.when(s == 5)
    def _conv_b():
        cx = cx_scr[...]
        conv = (convw_ref[0:1, :] * _shift_rows(cx, -1) + convw_ref[1:2, :] * cx
                + convw_ref[2:3, :] * _shift_rows(cx, 1))[HALO:HALO + tm]
        bg = _dot(h_scr[HALO:HALO + tm, :], win_ref[...])
        y_ref[:, 2 * D_BRANCH:] = (bg * conv).astype(BF16)


def _halo_specs(tm, n_rows):
    blocks_per_tile = tm // HALO
    last = n_rows // HALO - 1
    prev = pl.BlockSpec((HALO, D_MODEL), lambda i, s: (jnp.maximum(i * blocks_per_tile - 1, 0), 0))
    nxt = pl.BlockSpec((HALO, D_MODEL), lambda i, s: (jnp.minimum((i + 1) * blocks_per_tile, last), 0))
    return prev, nxt


def _mixer_in(x2d, layer, p, *, tm, seq_len):
    n_rows = x2d.shape[0]
    n_steps = len(_MIXER_IN_BLOCKS)

    def win_index(i, s):
        blk = jnp.int32(_MIXER_IN_BLOCKS[-1])
        for step in range(n_steps - 2, -1, -1):
            blk = jnp.where(s == step, _MIXER_IN_BLOCKS[step], blk)
        return (layer, blk, 0, 0)

    prev_spec, next_spec = _halo_specs(tm, n_rows)
    const = lambda *shape: pl.BlockSpec((None,) + shape, lambda i, s: (layer,) + (0,) * len(shape))
    kernel = functools.partial(_mixer_in_kernel, tm=tm, tiles_per_seq=seq_len // tm, seq_len=seq_len)
    return pl.pallas_call(
        kernel,
        grid=(n_rows // tm, n_steps),
        in_specs=[
            pl.BlockSpec((tm, D_MODEL), lambda i, s: (i, 0)),
            prev_spec, next_spec,
            const(1, D_MODEL),
            pl.BlockSpec((None, None, D_MODEL, D_BRANCH), win_index),
            const(len(POOL_WINDOWS), POOL_GROUP, POOL_GROUP),
            const(1, D_BRANCH),
            const(1, D_BRANCH),
            const(N_SGU_GROUPS, SGU_CHUNK, SGU_CHUNK),
            const(N_SGU_GROUPS, SGU_CHUNK, SGU_GROUP),
            const(3, D_BRANCH),
        ],
        out_specs=pl.BlockSpec((tm, 3 * D_BRANCH), lambda i, s: (i, 0)),
        out_shape=jax.ShapeDtypeStruct((n_rows, 3 * D_BRANCH), BF16),
        scratch_shapes=[
            pltpu.VMEM((tm + 2 * HALO, D_MODEL), BF16),
            pltpu.VMEM((tm + 2 * HALO, D_BRANCH), F32),
            pltpu.VMEM((tm, D_BRANCH), BF16),
            pltpu.VMEM((tm, D_BRANCH), F32),
        ],
        compiler_params=pltpu.CompilerParams(
            dimension_semantics=("parallel", "arbitrary"), vmem_limit_bytes=V7X_VMEM_LIMIT_BYTES),
        name="mixer_in",
    )(x2d, x2d, x2d, p["norm1_g"], p["w_in_branch"], p["w_pool"], p["pool_scale"], p["sgu_norm_g"],
      p["sgu_w"], p["sgu_b"], p["conv_w"])


def _mixer_out_kernel(x_ref, xres_ref, y_ref, g1_ref, wga_ref, wgb_ref, wgc_ref, wa_ref, wb_ref, wc_ref,
                      wo_ref, out_ref, h_scr, m_scr, *, n_merge, bn_merge):
    s = pl.program_id(1)

    @pl.when(s == 0)
    def _norm():
        h_scr[...] = _rmsnorm(x_ref[...], g1_ref[...]).astype(BF16)

    @pl.when(s < n_merge)
    def _merge():
        h = h_scr[...]
        merged = (jax.nn.sigmoid(_dot(h, wga_ref[...])) * _dot(y_ref[:, 0:D_BRANCH], wa_ref[...])
                  + jax.nn.sigmoid(_dot(h, wgb_ref[...])) * _dot(y_ref[:, D_BRANCH:2 * D_BRANCH], wb_ref[...])
                  + jax.nn.sigmoid(_dot(h, wgc_ref[...])) * _dot(y_ref[:, 2 * D_BRANCH:], wc_ref[...]))
        m_scr[s] = merged.astype(BF16)

    @pl.when(s >= n_merge)
    def _out():
        acc = xres_ref[...]
        for k in range(n_merge):
            acc = acc + _dot(m_scr[k], wo_ref[k * bn_merge:(k + 1) * bn_merge, :])
        out_ref[...] = acc


def _mixer_out(x2d, y, layer, p, *, tm, bn_merge, bn_out):
    n_rows = x2d.shape[0]
    n_merge = D_MODEL // bn_merge
    n_out = D_MODEL // bn_out

    def merge_blk(s):
        return jnp.minimum(s, n_merge - 1)

    def out_blk(s):
        return jnp.maximum(s - n_merge, 0)

    def gate_spec(branch):
        first = branch * n_merge
        return pl.BlockSpec((None, None, D_MODEL, bn_merge), lambda i, s: (layer, first + merge_blk(s), 0, 0))

    branch_spec = pl.BlockSpec((None, None, D_BRANCH, bn_merge), lambda i, s: (layer, merge_blk(s), 0, 0))
    kernel = functools.partial(_mixer_out_kernel, n_merge=n_merge, bn_merge=bn_merge)
    return pl.pallas_call(
        kernel,
        grid=(n_rows // tm, n_merge + n_out),
        in_specs=[
            pl.BlockSpec((tm, D_MODEL), lambda i, s: (i, 0)),
            pl.BlockSpec((tm, bn_out), lambda i, s: (i, out_blk(s))),
            pl.BlockSpec((tm, 3 * D_BRANCH), lambda i, s: (i, 0)),
            pl.BlockSpec((None, 1, D_MODEL), lambda i, s: (layer, 0, 0)),
            gate_spec(0), gate_spec(1), gate_spec(2),
            branch_spec, branch_spec, branch_spec,
            pl.BlockSpec((None, None, D_MODEL, bn_out), lambda i, s: (layer, out_blk(s), 0, 0)),
        ],
        out_specs=pl.BlockSpec((tm, bn_out), lambda i, s: (i, out_blk(s))),
        out_shape=jax.ShapeDtypeStruct((n_rows, D_MODEL), F32),
        scratch_shapes=[
            pltpu.VMEM((tm, D_MODEL), BF16),
            pltpu.VMEM((n_merge, tm, bn_merge), BF16),
        ],
        compiler_params=pltpu.CompilerParams(
            dimension_semantics=("parallel", "arbitrary"), vmem_limit_bytes=V7X_VMEM_LIMIT_BYTES),
        name="mixer_out",
    )(x2d, x2d, y, p["norm1_g"], p["w_in_gate"], p["w_in_gate"], p["w_in_gate"], p["w_pool_out"],
      p["w_sgu_out"], p["w_conv_out"], p["w_o"])


def _ffn_kernel(xm_ref, xp_ref, xn_ref, g2_ref, wg_ref, wv_ref, cwg_ref, cwv_ref, wd_ref, gf_ref,
                out_ref, h_scr, *, tm, tiles_per_seq, n_steps, final_norm):
    i = pl.program_id(0)
    s = pl.program_id(1)
    ti = i % tiles_per_seq

    @pl.when(s == 0)
    def _init():
        _norm_with_halo(h_scr, xm_ref, xp_ref, xn_ref, g2_ref[...], ti > 0, ti < tiles_per_seq - 1, tm)
        out_ref[...] = xm_ref[...]

    def conv3(up, cw_ref):
        return (cw_ref[0:1, :] * _shift_rows(up, -1) + cw_ref[1:2, :] * up
                + cw_ref[2:3, :] * _shift_rows(up, 1))[HALO:HALO + tm]

    h = h_scr[...]
    gate = conv3(_dot(h, wg_ref[...]), cwg_ref)
    val = conv3(_dot(h, wv_ref[...]), cwv_ref)
    act = (jax.nn.silu(gate) * val).astype(BF16)
    out_ref[...] += _dot(act, wd_ref[...])

    if final_norm:
        @pl.when(s == n_steps - 1)
        def _final():
            out_ref[...] = _rmsnorm(out_ref[...], gf_ref[...])


def _ffn(x2d, layer, p, final_g, *, tm, bn, seq_len, final_norm):
    n_rows = x2d.shape[0]
    n_steps = D_FF // bn
    prev_spec, next_spec = _halo_specs(tm, n_rows)
    kernel = functools.partial(_ffn_kernel, tm=tm, tiles_per_seq=seq_len // tm, n_steps=n_steps,
                               final_norm=final_norm)
    return pl.pallas_call(
        kernel,
        grid=(n_rows // tm, n_steps),
        in_specs=[
            pl.BlockSpec((tm, D_MODEL), lambda i, s: (i, 0)),
            prev_spec, next_spec,
            pl.BlockSpec((None, 1, D_MODEL), lambda i, s: (layer, 0, 0)),
            pl.BlockSpec((None, None, D_MODEL, bn), lambda i, s: (layer, s, 0, 0)),
            pl.BlockSpec((None, None, D_MODEL, bn), lambda i, s: (layer, n_steps + s, 0, 0)),
            pl.BlockSpec((None, 3, bn), lambda i, s: (layer, 0, s)),
            pl.BlockSpec((None, 3, bn), lambda i, s: (layer, 0, n_steps + s)),
            pl.BlockSpec((None, bn, D_MODEL), lambda i, s: (layer, s, 0)),
            pl.BlockSpec((1, D_MODEL), lambda i, s: (0, 0)),
        ],
        out_specs=pl.BlockSpec((tm, D_MODEL), lambda i, s: (i, 0)),
        out_shape=jax.ShapeDtypeStruct((n_rows, D_MODEL), F32),
        scratch_shapes=[pltpu.VMEM((tm + 2 * HALO, D_MODEL), BF16)],
        compiler_params=pltpu.CompilerParams(
            dimension_semantics=("parallel", "arbitrary"), vmem_limit_bytes=V7X_VMEM_LIMIT_BYTES),
        name="ffn",
    )(x2d, x2d, x2d, p["norm2_g"], p["w_up"], p["w_up"], p["ffn_conv_w"], p["ffn_conv_w"], p["w_down"],
      final_g)


def kernel(x, norm1_g, w_in, w_pool, pool_scale, sgu_norm_g, sgu_w, sgu_b, conv_w, w_pool_out, w_sgu_out,
           w_conv_out, w_o, norm2_g, w_up, ffn_conv_w, w_down, final_g):
    bsz, seq_len, d_model = x.shape
    depth = w_in.shape[0]
    assert d_model == D_MODEL and w_in.shape[2] == OFF_GATE + 3 * D_MODEL and w_up.shape[2] == 2 * D_FF
    tm = 512
    bn_merge = bn_out = bn_ff = 512
    assert seq_len % tm == 0 and tm % SGU_CHUNK == 0

    def col_blocks(w, bn):
        d, k, n = w.shape
        return jnp.transpose(w.astype(BF16).reshape(d, k, n // bn, bn), (0, 2, 1, 3))

    p = {
        "norm1_g": norm1_g[:, None, :], "norm2_g": norm2_g[:, None, :],
        "pool_scale": pool_scale[:, None, :], "sgu_norm_g": sgu_norm_g[:, None, :],
        "w_in_branch": col_blocks(w_in[:, :, :OFF_GATE], D_BRANCH),
        "w_in_gate": col_blocks(w_in[:, :, OFF_GATE:], bn_merge),
        "w_pool": w_pool.astype(BF16), "sgu_w": sgu_w.astype(BF16),
        "sgu_b": jnp.broadcast_to(jnp.transpose(sgu_b, (0, 2, 1))[..., None],
                                  (depth, N_SGU_GROUPS, SGU_CHUNK, SGU_GROUP)),
        "conv_w": conv_w, "ffn_conv_w": ffn_conv_w,
        "w_pool_out": col_blocks(w_pool_out, bn_merge), "w_sgu_out": col_blocks(w_sgu_out, bn_merge),
        "w_conv_out": col_blocks(w_conv_out, bn_merge), "w_o": col_blocks(w_o, bn_out),
        "w_up": col_blocks(w_up, bn_ff), "w_down": w_down.astype(BF16),
    }
    x2d = x.reshape(bsz * seq_len, d_model)
    for layer in range(depth):
        y = _mixer_in(x2d, layer, p, tm=tm, seq_len=seq_len)
        x2d = _mixer_out(x2d, y, layer, p, tm=tm, bn_merge=bn_merge, bn_out=bn_out)
        x2d = _ffn(x2d, layer, p, final_g[None, :], tm=tm, bn=bn_ff, seq_len=seq_len,
                   final_norm=(layer == depth - 1))
    return x2d.reshape(bsz, seq_len, d_model)
```

```python
import functools

import jax
import jax.numpy as jnp
from jax import lax
from jax.experimental import pallas as pl
from jax.experimental.pallas import tpu as pltpu

D_MODEL = 2048
D_BRANCH = 1024
POOL_WINDOWS = (2, 4, 8, 16)
POOL_GROUP = D_BRANCH // len(POOL_WINDOWS)
SGU_CHUNK = 128
N_SGU_GROUPS = 8
SGU_GROUP = D_BRANCH // N_SGU_GROUPS
D_FF = 5632
EPS = 1e-6
OFF_GATE = 6 * D_BRANCH

HALO = 16
V7X_VMEM_LIMIT_BYTES = 58 * 1024 * 1024

F32 = jnp.float32
BF16 = jnp.bfloat16


def _rmsnorm(xf, g):
    return xf * lax.rsqrt(jnp.mean(xf * xf, axis=-1, keepdims=True) + EPS) * g


def _dot(a, b):
    return jnp.dot(a, b, preferred_element_type=F32)


def _shift_rows(x, k):
    n = x.shape[0]
    return pltpu.roll(x, (-k) % n, axis=0)


def _norm_with_halo(h_scr, x_main, x_prev, x_next, g, has_prev, has_next, tm):
    h_scr[0:HALO, :] = jnp.where(has_prev, _rmsnorm(x_prev, g), 0.0).astype(BF16)
    h_scr[HALO:HALO + tm, :] = _rmsnorm(x_main, g).astype(BF16)
    h_scr[HALO + tm:, :] = jnp.where(has_next, _rmsnorm(x_next, g), 0.0).astype(BF16)


def _halo_specs(tm, n_rows):
    blocks_per_tile = tm // HALO
    last = n_rows // HALO - 1
    prev = pl.BlockSpec((HALO, D_MODEL), lambda i, s: (jnp.maximum(i * blocks_per_tile - 1, 0), 0))
    nxt = pl.BlockSpec((HALO, D_MODEL), lambda i, s: (jnp.minimum((i + 1) * blocks_per_tile, last), 0))
    return prev, nxt


_MIXER_IN_BLOCKS = (0, 2, 1, 3, 5, 4)


def _mixer_in_kernel(xm_ref, xp_ref, xn_ref, g1_ref, win_ref, wpool_ref, pscale_ref, sgug_ref,
                     sguw_ref, sgub_ref, convw_ref, y_ref, h_ref, h_scr, cx_scr, vn_scr, z_scr,
                     *, tm, tiles_per_seq, seq_len):
    i = pl.program_id(0)
    s = pl.program_id(1)
    ti = i % tiles_per_seq

    @pl.when(s == 0)
    def _pool():
        _norm_with_halo(h_scr, xm_ref[...], xp_ref[...], xn_ref[...], g1_ref[...],
                        ti > 0, ti < tiles_per_seq - 1, tm)
        h_ref[...] = h_scr[HALO:HALO + tm, :]
        a = _dot(h_scr[...], win_ref[...])
        pos = ti * tm + lax.broadcasted_iota(jnp.int32, (tm, POOL_GROUP), 0)
        for gi, w in enumerate(POOL_WINDOWS):
            cols = slice(gi * POOL_GROUP, (gi + 1) * POOL_GROUP)
            xg = a[:, cols]
            q, m = xg, 1
            while m < w // 2:
                q = q + _shift_rows(q, m)
                m *= 2
            win_sum = (q + _shift_rows(q, -(w // 2)))[HALO:HALO + tm]
            lo = jnp.maximum(pos - w // 2, 0)
            hi = jnp.minimum(pos + (w - w // 2 - 1), seq_len - 1)
            cnt = (hi - lo + 1).astype(F32)
            pooled = win_sum / cnt - xg[HALO:HALO + tm]
            y = _dot(pooled.astype(BF16), wpool_ref[gi]) * pscale_ref[:, cols]
            y_ref[:, cols] = y.astype(BF16)

    @pl.when(s == 1)
    def _sgu_v():
        gv = jax.nn.gelu(_dot(h_scr[HALO:HALO + tm, :], win_ref[...]))
        vn_scr[...] = _rmsnorm(gv, sgug_ref[...]).astype(BF16)
        n_chunk = tm // SGU_CHUNK
        for g in range(N_SGU_GROUPS):
            cols = slice(g * SGU_GROUP, (g + 1) * SGU_GROUP)
            rhs = jnp.concatenate(
                [vn_scr[c * SGU_CHUNK:(c + 1) * SGU_CHUNK, cols] for c in range(n_chunk)], axis=1)
            o = _dot(sguw_ref[g], rhs)
            for c in range(n_chunk):
                z_scr[c * SGU_CHUNK:(c + 1) * SGU_CHUNK, cols] = (
                    o[:, c * SGU_GROUP:(c + 1) * SGU_GROUP] + sgub_ref[g])

    @pl.when(s == 2)
    def _sgu_u():
        u = jax.nn.gelu(_dot(h_scr[HALO:HALO + tm, :], win_ref[...]))
        y_ref[:, D_BRANCH:2 * D_BRANCH] = (u * z_scr[...]).astype(BF16)

    @pl.when(s == 3)
    def _conv_x():
        cx_scr[...] = _dot(h_scr[...], win_ref[...])

    @pl.when(s == 4)
    def _conv_c():
        cx_scr[...] = cx_scr[...] * _dot(h_scr[...], win_ref[...])

    @pl.when(s == 5)
    def _conv_b():
        cx = cx_scr[...]
        conv = (convw_ref[0:1, :] * _shift_rows(cx, -1) + convw_ref[1:2, :] * cx
                + convw_ref[2:3, :] * _shift_rows(cx, 1))[HALO:HALO + tm]
        bg = _dot(h_scr[HALO:HALO + tm, :], win_ref[...])
        y_ref[:, 2 * D_BRANCH:] = (bg * conv).astype(BF16)


def _mixer_in(x2d, layer, p, *, tm, seq_len):
    n_rows = x2d.shape[0]
    n_steps = len(_MIXER_IN_BLOCKS)

    def win_index(i, s):
        blk = jnp.int32(_MIXER_IN_BLOCKS[-1])
        for step in range(n_steps - 2, -1, -1):
            blk = jnp.where(s == step, _MIXER_IN_BLOCKS[step], blk)
        return (layer, 0, blk)

    prev_spec, next_spec = _halo_specs(tm, n_rows)
    const = lambda *shape: pl.BlockSpec((None,) + shape, lambda i, s: (layer,) + (0,) * len(shape))
    kernel = functools.partial(_mixer_in_kernel, tm=tm, tiles_per_seq=seq_len // tm, seq_len=seq_len)
    return pl.pallas_call(
        kernel,
        grid=(n_rows // tm, n_steps),
        in_specs=[
            pl.BlockSpec((tm, D_MODEL), lambda i, s: (i, 0)),
            prev_spec, next_spec,
            const(1, D_MODEL),
            pl.BlockSpec((None, D_MODEL, D_BRANCH), win_index),
            const(len(POOL_WINDOWS), POOL_GROUP, POOL_GROUP),
            const(1, D_BRANCH),
            const(1, D_BRANCH),
            const(N_SGU_GROUPS, SGU_CHUNK, SGU_CHUNK),
            const(N_SGU_GROUPS, SGU_CHUNK, SGU_GROUP),
            const(3, D_BRANCH),
        ],
        out_specs=[pl.BlockSpec((tm, 3 * D_BRANCH), lambda i, s: (i, 0)),
                   pl.BlockSpec((tm, D_MODEL), lambda i, s: (i, 0))],
        out_shape=[jax.ShapeDtypeStruct((n_rows, 3 * D_BRANCH), BF16),
                   jax.ShapeDtypeStruct((n_rows, D_MODEL), BF16)],
        scratch_shapes=[
            pltpu.VMEM((tm + 2 * HALO, D_MODEL), BF16),
            pltpu.VMEM((tm + 2 * HALO, D_BRANCH), F32),
            pltpu.VMEM((tm, D_BRANCH), BF16),
            pltpu.VMEM((tm, D_BRANCH), F32),
        ],
        compiler_params=pltpu.CompilerParams(
            dimension_semantics=("parallel", "arbitrary"), vmem_limit_bytes=V7X_VMEM_LIMIT_BYTES),
        name="mixer_in",
    )(x2d, x2d, x2d, p["norm1_g"], p["w_in"], p["w_pool"], p["pool_scale"], p["sgu_norm_g"],
      p["sgu_w"], p["sgu_b"], p["conv_w"])


def _mixer_out_kernel(h_ref, y_ref, wga_ref, wgb_ref, wgc_ref, wa_ref, wb_ref, wc_ref, wo_ref, out_ref,
                      *, bn_acc):
    s = pl.program_id(1)

    def step(first):
        h = h_ref[...]
        merged = (jax.nn.sigmoid(_dot(h, wga_ref[...])) * _dot(y_ref[:, 0:D_BRANCH], wa_ref[...])
                  + jax.nn.sigmoid(_dot(h, wgb_ref[...])) * _dot(y_ref[:, D_BRANCH:2 * D_BRANCH], wb_ref[...])
                  + jax.nn.sigmoid(_dot(h, wgc_ref[...])) * _dot(y_ref[:, 2 * D_BRANCH:], wc_ref[...]))
        merged = merged.astype(BF16)
        for c in range(D_MODEL // bn_acc):
            cols = slice(c * bn_acc, (c + 1) * bn_acc)
            upd = _dot(merged, wo_ref[:, cols])
            if first:
                out_ref[:, cols] = upd
            else:
                out_ref[:, cols] += upd

    pl.when(s == 0)(functools.partial(step, True))
    pl.when(s > 0)(functools.partial(step, False))


def _mixer_out(h, y, layer, p, *, tm, bn):
    n_rows = h.shape[0]
    n_steps = D_MODEL // bn

    def gate_spec(branch):
        first = (OFF_GATE + branch * D_MODEL) // bn
        return pl.BlockSpec((None, D_MODEL, bn), lambda i, s: (layer, 0, first + s))

    branch_spec = pl.BlockSpec((None, D_BRANCH, bn), lambda i, s: (layer, 0, s))
    kernel = functools.partial(_mixer_out_kernel, bn_acc=512)
    return pl.pallas_call(
        kernel,
        grid=(n_rows // tm, n_steps),
        in_specs=[
            pl.BlockSpec((tm, D_MODEL), lambda i, s: (i, 0)),
            pl.BlockSpec((tm, 3 * D_BRANCH), lambda i, s: (i, 0)),
            gate_spec(0), gate_spec(1), gate_spec(2),
            branch_spec, branch_spec, branch_spec,
            pl.BlockSpec((None, bn, D_MODEL), lambda i, s: (layer, s, 0)),
        ],
        out_specs=pl.BlockSpec((tm, D_MODEL), lambda i, s: (i, 0)),
        out_shape=jax.ShapeDtypeStruct((n_rows, D_MODEL), F32),
        compiler_params=pltpu.CompilerParams(
            dimension_semantics=("parallel", "arbitrary"), vmem_limit_bytes=V7X_VMEM_LIMIT_BYTES),
        name="mixer_out",
    )(h, y, p["w_in"], p["w_in"], p["w_in"], p["w_pool_out"], p["w_sgu_out"], p["w_conv_out"], p["w_o"])


def _ffn_kernel(xm_ref, xp_ref, xn_ref, dm_ref, dp_ref, dn_ref, g2_ref, wg_ref, wv_ref, cwg_ref, cwv_ref,
                wd_ref, gf_ref, out_ref, h_scr, *, tm, tiles_per_seq, n_steps, final_norm):
    i = pl.program_id(0)
    s = pl.program_id(1)
    ti = i % tiles_per_seq

    @pl.when(s == 0)
    def _init():
        x_mid = xm_ref[...] + dm_ref[...]
        _norm_with_halo(h_scr, x_mid, xp_ref[...] + dp_ref[...], xn_ref[...] + dn_ref[...], g2_ref[...],
                        ti > 0, ti < tiles_per_seq - 1, tm)
        out_ref[...] = x_mid

    def conv3(up, cw_ref):
        return (cw_ref[0:1, :] * _shift_rows(up, -1) + cw_ref[1:2, :] * up
                + cw_ref[2:3, :] * _shift_rows(up, 1))[HALO:HALO + tm]

    h = h_scr[...]
    gate = conv3(_dot(h, wg_ref[...]), cwg_ref)
    val = conv3(_dot(h, wv_ref[...]), cwv_ref)
    act = (jax.nn.silu(gate) * val).astype(BF16)
    out_ref[...] += _dot(act, wd_ref[...])

    if final_norm:
        @pl.when(s == n_steps - 1)
        def _final():
            out_ref[...] = _rmsnorm(out_ref[...], gf_ref[...])


def _ffn(x2d, delta, layer, p, final_g, *, tm, bn, seq_len, final_norm):
    n_rows = x2d.shape[0]
    n_steps = D_FF // bn
    prev_spec, next_spec = _halo_specs(tm, n_rows)
    main_spec = pl.BlockSpec((tm, D_MODEL), lambda i, s: (i, 0))
    kernel = functools.partial(_ffn_kernel, tm=tm, tiles_per_seq=seq_len // tm, n_steps=n_steps,
                               final_norm=final_norm)
    return pl.pallas_call(
        kernel,
        grid=(n_rows // tm, n_steps),
        in_specs=[
            main_spec, prev_spec, next_spec,
            main_spec, prev_spec, next_spec,
            pl.BlockSpec((None, 1, D_MODEL), lambda i, s: (layer, 0, 0)),
            pl.BlockSpec((None, D_MODEL, bn), lambda i, s: (layer, 0, s)),
            pl.BlockSpec((None, D_MODEL, bn), lambda i, s: (layer, 0, n_steps + s)),
            pl.BlockSpec((None, 3, bn), lambda i, s: (layer, 0, s)),
            pl.BlockSpec((None, 3, bn), lambda i, s: (layer, 0, n_steps + s)),
            pl.BlockSpec((None, bn, D_MODEL), lambda i, s: (layer, s, 0)),
            pl.BlockSpec((1, D_MODEL), lambda i, s: (0, 0)),
        ],
        out_specs=main_spec,
        out_shape=jax.ShapeDtypeStruct((n_rows, D_MODEL), F32),
        scratch_shapes=[pltpu.VMEM((tm + 2 * HALO, D_MODEL), BF16)],
        compiler_params=pltpu.CompilerParams(
            dimension_semantics=("parallel", "arbitrary"), vmem_limit_bytes=V7X_VMEM_LIMIT_BYTES),
        name="ffn",
    )(x2d, x2d, x2d, delta, delta, delta, p["norm2_g"], p["w_up"], p["w_up"], p["ffn_conv_w"],
      p["ffn_conv_w"], p["w_down"], final_g)


def kernel(x, norm1_g, w_in, w_pool, pool_scale, sgu_norm_g, sgu_w, sgu_b, conv_w, w_pool_out, w_sgu_out,
           w_conv_out, w_o, norm2_g, w_up, ffn_conv_w, w_down, final_g):
    bsz, seq_len, d_model = x.shape
    depth = w_in.shape[0]
    assert d_model == D_MODEL and w_in.shape[2] == OFF_GATE + 3 * D_MODEL and w_up.shape[2] == 2 * D_FF
    tm_in, tm_out, tm_ffn = 512, 1024, 512
    assert all(seq_len % t == 0 and t % SGU_CHUNK == 0 for t in (tm_in, tm_out, tm_ffn))

    p = {
        "norm1_g": norm1_g[:, None, :], "norm2_g": norm2_g[:, None, :],
        "pool_scale": pool_scale[:, None, :], "sgu_norm_g": sgu_norm_g[:, None, :],
        "w_in": w_in.astype(BF16), "w_pool": w_pool.astype(BF16), "sgu_w": sgu_w.astype(BF16),
        "sgu_b": jnp.broadcast_to(jnp.transpose(sgu_b, (0, 2, 1))[..., None],
                                  (depth, N_SGU_GROUPS, SGU_CHUNK, SGU_GROUP)),
        "conv_w": conv_w, "ffn_conv_w": ffn_conv_w,
        "w_pool_out": w_pool_out.astype(BF16), "w_sgu_out": w_sgu_out.astype(BF16),
        "w_conv_out": w_conv_out.astype(BF16), "w_o": w_o.astype(BF16),
        "w_up": w_up.astype(BF16), "w_down": w_down.astype(BF16),
    }
    x2d = x.reshape(bsz * seq_len, d_model)
    for layer in range(depth):
        y, h = _mixer_in(x2d, layer, p, tm=tm_in, seq_len=seq_len)
        delta = _mixer_out(h, y, layer, p, tm=tm_out, bn=256)
        x2d = _ffn(x2d, delta, layer, p, final_g[None, :], tm=tm_ffn, bn=512, seq_len=seq_len,
                   final_norm=(layer == depth - 1))
    return x2d.reshape(bsz, seq_len, d_model)
```

```python
import functools

import jax
import jax.numpy as jnp
from jax import lax
from jax.experimental import pallas as pl
from jax.experimental.pallas import tpu as pltpu

D_MODEL = 2048
D_BRANCH = 1024
POOL_WINDOWS = (2, 4, 8, 16)
POOL_GROUP = D_BRANCH // len(POOL_WINDOWS)
SGU_CHUNK = 128
N_SGU_GROUPS = 8
SGU_GROUP = D_BRANCH // N_SGU_GROUPS
D_FF = 5632
EPS = 1e-6
OFF_GATE = 6 * D_BRANCH

HALO = 16
MXU_ACC_ROWS = 512
V7X_VMEM_LIMIT_BYTES = 58 * 1024 * 1024

F32 = jnp.float32
BF16 = jnp.bfloat16


def _rmsnorm(xf, g):
    return xf * lax.rsqrt(jnp.mean(xf * xf, axis=-1, keepdims=True) + EPS) * g


def _dot(a, b):
    return jnp.dot(a, b, preferred_element_type=F32)


def _shift_rows(x, k):
    n = x.shape[0]
    return pltpu.roll(x, (-k) % n, axis=0)


def _norm_with_halo(h_scr, x_main, x_prev, x_next, g, has_prev, has_next, tm):
    h_scr[0:HALO, :] = jnp.where(has_prev, _rmsnorm(x_prev, g), 0.0).astype(BF16)
    h_scr[HALO:HALO + tm, :] = _rmsnorm(x_main, g).astype(BF16)
    h_scr[HALO + tm:, :] = jnp.where(has_next, _rmsnorm(x_next, g), 0.0).astype(BF16)


def _first_step_tile_spec(tm, n_rows):
    last = n_rows // tm - 1
    return pl.BlockSpec((tm, D_MODEL), lambda i, s: (jnp.minimum(i + jnp.minimum(s, 1), last), 0))


def _halo_specs(tm, n_rows):
    blocks_per_tile = tm // HALO
    last = n_rows // HALO - 1
    prev = pl.BlockSpec((HALO, D_MODEL), lambda i, s: (jnp.maximum(i * blocks_per_tile - 1, 0), 0))
    nxt = pl.BlockSpec((HALO, D_MODEL), lambda i, s: (jnp.minimum((i + 1) * blocks_per_tile, last), 0))
    return prev, nxt


_MIXER_IN_BLOCKS = (0, 2, 1, 3, 5, 4)


def _mixer_in_kernel(xm_ref, xp_ref, xn_ref, g1_ref, win_ref, wpool_ref, pscale_ref, sgug_ref,
                     sguw_ref, sgub_ref, convw_ref, y_ref, h_ref, h_scr, cx_scr, vn_scr, z_scr,
                     *, tm, tiles_per_seq, seq_len):
    i = pl.program_id(0)
    s = pl.program_id(1)
    ti = i % tiles_per_seq

    @pl.when(s == 0)
    def _pool():
        _norm_with_halo(h_scr, xm_ref[...], xp_ref[...], xn_ref[...], g1_ref[...],
                        ti > 0, ti < tiles_per_seq - 1, tm)
        h_ref[...] = h_scr[HALO:HALO + tm, :]
        a = _dot(h_scr[...], win_ref[...])
        pos = ti * tm + lax.broadcasted_iota(jnp.int32, (tm, POOL_GROUP), 0)
        for gi, w in enumerate(POOL_WINDOWS):
            cols = slice(gi * POOL_GROUP, (gi + 1) * POOL_GROUP)
            xg = a[:, cols]
            q, m = xg, 1
            while m < w // 2:
                q = q + _shift_rows(q, m)
                m *= 2
            win_sum = (q + _shift_rows(q, -(w // 2)))[HALO:HALO + tm]
            lo = jnp.maximum(pos - w // 2, 0)
            hi = jnp.minimum(pos + (w - w // 2 - 1), seq_len - 1)
            cnt = (hi - lo + 1).astype(F32)
            pooled = win_sum / cnt - xg[HALO:HALO + tm]
            y = _dot(pooled.astype(BF16), wpool_ref[gi]) * pscale_ref[:, cols]
            y_ref[:, cols] = y.astype(BF16)

    @pl.when(s == 1)
    def _sgu_v():
        gv = jax.nn.gelu(_dot(h_scr[HALO:HALO + tm, :], win_ref[...]))
        vn_scr[...] = _rmsnorm(gv, sgug_ref[...]).astype(BF16)
        n_chunk = tm // SGU_CHUNK
        for g in range(N_SGU_GROUPS):
            cols = slice(g * SGU_GROUP, (g + 1) * SGU_GROUP)
            rhs = jnp.concatenate(
                [vn_scr[c * SGU_CHUNK:(c + 1) * SGU_CHUNK, cols] for c in range(n_chunk)], axis=1)
            o = _dot(sguw_ref[g], rhs)
            for c in range(n_chunk):
                z_scr[c * SGU_CHUNK:(c + 1) * SGU_CHUNK, cols] = (
                    o[:, c * SGU_GROUP:(c + 1) * SGU_GROUP] + sgub_ref[g])

    @pl.when(s == 2)
    def _sgu_u():
        u = jax.nn.gelu(_dot(h_scr[HALO:HALO + tm, :], win_ref[...]))
        y_ref[:, D_BRANCH:2 * D_BRANCH] = (u * z_scr[...]).astype(BF16)

    @pl.when(s == 3)
    def _conv_x():
        cx_scr[...] = _dot(h_scr[...], win_ref[...])

    @pl.when(s == 4)
    def _conv_c():
        cx_scr[...] = cx_scr[...] * _dot(h_scr[...], win_ref[...])

    @pl.when(s == 5)
    def _conv_b():
        cx = cx_scr[...]
        conv = (convw_ref[0:1, :] * _shift_rows(cx, -1) + convw_ref[1:2, :] * cx
                + convw_ref[2:3, :] * _shift_rows(cx, 1))[HALO:HALO + tm]
        bg = _dot(h_scr[HALO:HALO + tm, :], win_ref[...])
        y_ref[:, 2 * D_BRANCH:] = (bg * conv).astype(BF16)


def _mixer_in(x2d, layer, p, *, tm, seq_len):
    n_rows = x2d.shape[0]
    n_steps = len(_MIXER_IN_BLOCKS)

    def win_index(i, s):
        blk = jnp.int32(_MIXER_IN_BLOCKS[-1])
        for step in range(n_steps - 2, -1, -1):
            blk = jnp.where(s == step, _MIXER_IN_BLOCKS[step], blk)
        return (layer, 0, blk)

    prev_spec, next_spec = _halo_specs(tm, n_rows)
    const = lambda *shape: pl.BlockSpec((None,) + shape, lambda i, s: (layer,) + (0,) * len(shape))
    kernel = functools.partial(_mixer_in_kernel, tm=tm, tiles_per_seq=seq_len // tm, seq_len=seq_len)
    return pl.pallas_call(
        kernel,
        grid=(n_rows // tm, n_steps),
        in_specs=[
            _first_step_tile_spec(tm, n_rows),
            prev_spec, next_spec,
            const(1, D_MODEL),
            pl.BlockSpec((None, D_MODEL, D_BRANCH), win_index),
            const(len(POOL_WINDOWS), POOL_GROUP, POOL_GROUP),
            const(1, D_BRANCH),
            const(1, D_BRANCH),
            const(N_SGU_GROUPS, SGU_CHUNK, SGU_CHUNK),
            const(N_SGU_GROUPS, SGU_CHUNK, SGU_GROUP),
            const(3, D_BRANCH),
        ],
        out_specs=[pl.BlockSpec((tm, 3 * D_BRANCH), lambda i, s: (i, 0)),
                   pl.BlockSpec((tm, D_MODEL), lambda i, s: (i, 0))],
        out_shape=[jax.ShapeDtypeStruct((n_rows, 3 * D_BRANCH), BF16),
                   jax.ShapeDtypeStruct((n_rows, D_MODEL), BF16)],
        scratch_shapes=[
            pltpu.VMEM((tm + 2 * HALO, D_MODEL), BF16),
            pltpu.VMEM((tm + 2 * HALO, D_BRANCH), F32),
            pltpu.VMEM((tm, D_BRANCH), BF16),
            pltpu.VMEM((tm, D_BRANCH), F32),
        ],
        compiler_params=pltpu.CompilerParams(
            dimension_semantics=("parallel", "arbitrary"), vmem_limit_bytes=V7X_VMEM_LIMIT_BYTES),
        name="mixer_in",
    )(x2d, x2d, x2d, p["norm1_g"], p["w_in"], p["w_pool"], p["pool_scale"], p["sgu_norm_g"],
      p["sgu_w"], p["sgu_b"], p["conv_w"])


def _mixer_out_kernel(h_ref, y_ref, wga_ref, wgb_ref, wgc_ref, wa_ref, wb_ref, wc_ref, wo_ref, out_ref,
                      *, bn_acc):
    s = pl.program_id(1)

    def step(first):
        for r0 in range(0, h_ref.shape[0], MXU_ACC_ROWS):
            rows = slice(r0, r0 + MXU_ACC_ROWS)
            h = h_ref[rows, :]
            merged = (jax.nn.sigmoid(_dot(h, wga_ref[...])) * _dot(y_ref[rows, 0:D_BRANCH], wa_ref[...])
                      + jax.nn.sigmoid(_dot(h, wgb_ref[...]))
                      * _dot(y_ref[rows, D_BRANCH:2 * D_BRANCH], wb_ref[...])
                      + jax.nn.sigmoid(_dot(h, wgc_ref[...])) * _dot(y_ref[rows, 2 * D_BRANCH:], wc_ref[...]))
            merged = merged.astype(BF16)
            for c in range(D_MODEL // bn_acc):
                cols = slice(c * bn_acc, (c + 1) * bn_acc)
                upd = _dot(merged, wo_ref[:, cols])
                if first:
                    out_ref[rows, cols] = upd
                else:
                    out_ref[rows, cols] += upd

    pl.when(s == 0)(functools.partial(step, True))
    pl.when(s > 0)(functools.partial(step, False))


def _mixer_out(h, y, layer, p, *, tm, bn):
    n_rows = h.shape[0]
    n_steps = D_MODEL // bn

    def gate_spec(branch):
        first = (OFF_GATE + branch * D_MODEL) // bn
        return pl.BlockSpec((None, D_MODEL, bn), lambda i, s: (layer, 0, first + s))

    branch_spec = pl.BlockSpec((None, D_BRANCH, bn), lambda i, s: (layer, 0, s))
    kernel = functools.partial(_mixer_out_kernel, bn_acc=512)
    return pl.pallas_call(
        kernel,
        grid=(n_rows // tm, n_steps),
        in_specs=[
            pl.BlockSpec((tm, D_MODEL), lambda i, s: (i, 0)),
            pl.BlockSpec((tm, 3 * D_BRANCH), lambda i, s: (i, 0)),
            gate_spec(0), gate_spec(1), gate_spec(2),
            branch_spec, branch_spec, branch_spec,
            pl.BlockSpec((None, bn, D_MODEL), lambda i, s: (layer, s, 0)),
        ],
        out_specs=pl.BlockSpec((tm, D_MODEL), lambda i, s: (i, 0)),
        out_shape=jax.ShapeDtypeStruct((n_rows, D_MODEL), F32),
        compiler_params=pltpu.CompilerParams(
            dimension_semantics=("parallel", "arbitrary"), vmem_limit_bytes=V7X_VMEM_LIMIT_BYTES),
        name="mixer_out",
    )(h, y, p["w_in"], p["w_in"], p["w_in"], p["w_pool_out"], p["w_sgu_out"], p["w_conv_out"], p["w_o"])


def _ffn_kernel(xm_ref, xp_ref, xn_ref, dm_ref, dp_ref, dn_ref, g2_ref, wg_ref, wv_ref, cwg_ref, cwv_ref,
                wd_ref, gf_ref, out_ref, h_scr, *, tm, tiles_per_seq, n_steps, final_norm):
    i = pl.program_id(0)
    s = pl.program_id(1)
    ti = i % tiles_per_seq

    @pl.when(s == 0)
    def _init():
        x_mid = xm_ref[...] + dm_ref[...]
        _norm_with_halo(h_scr, x_mid, xp_ref[...] + dp_ref[...], xn_ref[...] + dn_ref[...], g2_ref[...],
                        ti > 0, ti < tiles_per_seq - 1, tm)
        out_ref[...] = x_mid

    def conv3(up, cw_ref):
        return (cw_ref[0:1, :] * _shift_rows(up, -1) + cw_ref[1:2, :] * up
                + cw_ref[2:3, :] * _shift_rows(up, 1))[HALO:HALO + tm]

    h = h_scr[...]
    gate = conv3(_dot(h, wg_ref[...]), cwg_ref)
    val = conv3(_dot(h, wv_ref[...]), cwv_ref)
    act = (jax.nn.silu(gate) * val).astype(BF16)
    out_ref[...] += _dot(act, wd_ref[...])

    if final_norm:
        @pl.when(s == n_steps - 1)
        def _final():
            out_ref[...] = _rmsnorm(out_ref[...], gf_ref[...])


def _ffn(x2d, delta, layer, p, final_g, *, tm, bn, seq_len, final_norm):
    n_rows = x2d.shape[0]
    n_steps = D_FF // bn
    prev_spec, next_spec = _halo_specs(tm, n_rows)
    first_step_spec = _first_step_tile_spec(tm, n_rows)
    kernel = functools.partial(_ffn_kernel, tm=tm, tiles_per_seq=seq_len // tm, n_steps=n_steps,
                               final_norm=final_norm)
    return pl.pallas_call(
        kernel,
        grid=(n_rows // tm, n_steps),
        in_specs=[
            first_step_spec, prev_spec, next_spec,
            first_step_spec, prev_spec, next_spec,
            pl.BlockSpec((None, 1, D_MODEL), lambda i, s: (layer, 0, 0)),
            pl.BlockSpec((None, D_MODEL, bn), lambda i, s: (layer, 0, s)),
            pl.BlockSpec((None, D_MODEL, bn), lambda i, s: (layer, 0, n_steps + s)),
            pl.BlockSpec((None, 3, bn), lambda i, s: (layer, 0, s)),
            pl.BlockSpec((None, 3, bn), lambda i, s: (layer, 0, n_steps + s)),
            pl.BlockSpec((None, bn, D_MODEL), lambda i, s: (layer, s, 0)),
            pl.BlockSpec((1, D_MODEL), lambda i, s: (0, 0)),
        ],
        out_specs=pl.BlockSpec((tm, D_MODEL), lambda i, s: (i, 0)),
        out_shape=jax.ShapeDtypeStruct((n_rows, D_MODEL), F32),
        scratch_shapes=[pltpu.VMEM((tm + 2 * HALO, D_MODEL), BF16)],
        compiler_params=pltpu.CompilerParams(
            dimension_semantics=("parallel", "arbitrary"), vmem_limit_bytes=V7X_VMEM_LIMIT_BYTES),
        name="ffn",
    )(x2d, x2d, x2d, delta, delta, delta, p["norm2_g"], p["w_up"], p["w_up"], p["ffn_conv_w"],
      p["ffn_conv_w"], p["w_down"], final_g)


def kernel(x, norm1_g, w_in, w_pool, pool_scale, sgu_norm_g, sgu_w, sgu_b, conv_w, w_pool_out, w_sgu_out,
           w_conv_out, w_o, norm2_g, w_up, ffn_conv_w, w_down, final_g):
    bsz, seq_len, d_model = x.shape
    depth = w_in.shape[0]
    assert d_model == D_MODEL and w_in.shape[2] == OFF_GATE + 3 * D_MODEL and w_up.shape[2] == 2 * D_FF
    tm_in, tm_out, tm_ffn = 512, 1024, 512
    assert all(seq_len % t == 0 and t % SGU_CHUNK == 0 for t in (tm_in, tm_out, tm_ffn))

    p = {
        "norm1_g": norm1_g[:, None, :], "norm2_g": norm2_g[:, None, :],
        "pool_scale": pool_scale[:, None, :], "sgu_norm_g": sgu_norm_g[:, None, :],
        "w_in": w_in.astype(BF16), "w_pool": w_pool.astype(BF16), "sgu_w": sgu_w.astype(BF16),
        "sgu_b": jnp.broadcast_to(jnp.transpose(sgu_b, (0, 2, 1))[..., None],
                                  (depth, N_SGU_GROUPS, SGU_CHUNK, SGU_GROUP)),
        "conv_w": conv_w, "ffn_conv_w": ffn_conv_w,
        "w_pool_out": w_pool_out.astype(BF16), "w_sgu_out": w_sgu_out.astype(BF16),
        "w_conv_out": w_conv_out.astype(BF16), "w_o": w_o.astype(BF16),
        "w_up": w_up.astype(BF16), "w_down": w_down.astype(BF16),
    }
    x2d = x.reshape(bsz * seq_len, d_model)
    for layer in range(depth):
        y, h = _mixer_in(x2d, layer, p, tm=tm_in, seq_len=seq_len)
        delta = _mixer_out(h, y, layer, p, tm=tm_out, bn=256)
        x2d = _ffn(x2d, delta, layer, p, final_g[None, :], tm=tm_ffn, bn=512, seq_len=seq_len,
                   final_norm=(layer == depth - 1))
    return x2d.reshape(bsz, seq_len, d_model)
```

```python
import functools

import jax
import jax.numpy as jnp
from jax import lax
from jax.experimental import pallas as pl
from jax.experimental.pallas import tpu as pltpu

D_MODEL = 2048
D_BRANCH = 1024
POOL_WINDOWS = (2, 4, 8, 16)
POOL_GROUP = D_BRANCH // len(POOL_WINDOWS)
SGU_CHUNK = 128
N_SGU_GROUPS = 8
SGU_GROUP = D_BRANCH // N_SGU_GROUPS
D_FF = 5632
EPS = 1e-6
OFF_GATE = 6 * D_BRANCH

HALO = 16
MXU_ACC_ROWS = 512
V7X_VMEM_LIMIT_BYTES = 58 * 1024 * 1024

F32 = jnp.float32
BF16 = jnp.bfloat16


def _rmsnorm(xf, g):
    return xf * lax.rsqrt(jnp.mean(xf * xf, axis=-1, keepdims=True) + EPS) * g


def _dot(a, b):
    return jnp.dot(a, b, preferred_element_type=F32)


def _shift_rows(x, k):
    n = x.shape[0]
    return pltpu.roll(x, (-k) % n, axis=0)


def _norm_with_halo(h_scr, x_main, x_prev, x_next, g, has_prev, has_next, tm):
    h_scr[0:HALO, :] = jnp.where(has_prev, _rmsnorm(x_prev, g), 0.0).astype(BF16)
    h_scr[HALO:HALO + tm, :] = _rmsnorm(x_main, g).astype(BF16)
    h_scr[HALO + tm:, :] = jnp.where(has_next, _rmsnorm(x_next, g), 0.0).astype(BF16)


def _halo_specs(tm, n_rows):
    blocks_per_tile = tm // HALO
    last = n_rows // HALO - 1
    prev = pl.BlockSpec((HALO, D_MODEL), lambda i, s: (jnp.maximum(i * blocks_per_tile - 1, 0), 0))
    nxt = pl.BlockSpec((HALO, D_MODEL), lambda i, s: (jnp.minimum((i + 1) * blocks_per_tile, last), 0))
    return prev, nxt


_MIXER_IN_BLOCKS = (0, 2, 1, 3, 5, 4)


def _mixer_in_kernel(xm_ref, xp_ref, xn_ref, g1_ref, win_ref, wpool_ref, pscale_ref, sgug_ref,
                     sguw_ref, sgub_ref, convw_ref, y_ref, h_ref, h_scr, cx_scr, vn_scr, z_scr,
                     *, tm, tiles_per_seq, seq_len):
    i = pl.program_id(0)
    s = pl.program_id(1)
    ti = i % tiles_per_seq

    @pl.when(s == 0)
    def _pool():
        _norm_with_halo(h_scr, xm_ref[...], xp_ref[...], xn_ref[...], g1_ref[...],
                        ti > 0, ti < tiles_per_seq - 1, tm)
        h_ref[...] = h_scr[HALO:HALO + tm, :]
        a = _dot(h_scr[...], win_ref[...])
        pos = ti * tm + lax.broadcasted_iota(jnp.int32, (tm, POOL_GROUP), 0)
        for gi, w in enumerate(POOL_WINDOWS):
            cols = slice(gi * POOL_GROUP, (gi + 1) * POOL_GROUP)
            xg = a[:, cols]
            q, m = xg, 1
            while m < w // 2:
                q = q + _shift_rows(q, m)
                m *= 2
            win_sum = (q + _shift_rows(q, -(w // 2)))[HALO:HALO + tm]
            lo = jnp.maximum(pos - w // 2, 0)
            hi = jnp.minimum(pos + (w - w // 2 - 1), seq_len - 1)
            cnt = (hi - lo + 1).astype(F32)
            pooled = win_sum / cnt - xg[HALO:HALO + tm]
            y = _dot(pooled.astype(BF16), wpool_ref[gi]) * pscale_ref[:, cols]
            y_ref[:, cols] = y.astype(BF16)

    @pl.when(s == 1)
    def _sgu_v():
        gv = jax.nn.gelu(_dot(h_scr[HALO:HALO + tm, :], win_ref[...]))
        vn_scr[...] = _rmsnorm(gv, sgug_ref[...]).astype(BF16)
        n_chunk = tm // SGU_CHUNK
        for g in range(N_SGU_GROUPS):
            cols = slice(g * SGU_GROUP, (g + 1) * SGU_GROUP)
            rhs = jnp.concatenate(
                [vn_scr[c * SGU_CHUNK:(c + 1) * SGU_CHUNK, cols] for c in range(n_chunk)], axis=1)
            o = _dot(sguw_ref[g], rhs)
            for c in range(n_chunk):
                z_scr[c * SGU_CHUNK:(c + 1) * SGU_CHUNK, cols] = (
                    o[:, c * SGU_GROUP:(c + 1) * SGU_GROUP] + sgub_ref[g])

    @pl.when(s == 2)
    def _sgu_u():
        u = jax.nn.gelu(_dot(h_scr[HALO:HALO + tm, :], win_ref[...]))
        y_ref[:, D_BRANCH:2 * D_BRANCH] = (u * z_scr[...]).astype(BF16)

    @pl.when(s == 3)
    def _conv_x():
        cx_scr[...] = _dot(h_scr[...], win_ref[...])

    @pl.when(s == 4)
    def _conv_c():
        cx_scr[...] = cx_scr[...] * _dot(h_scr[...], win_ref[...])

    @pl.when(s == 5)
    def _conv_b():
        cx = cx_scr[...]
        conv = (convw_ref[0:1, :] * _shift_rows(cx, -1) + convw_ref[1:2, :] * cx
                + convw_ref[2:3, :] * _shift_rows(cx, 1))[HALO:HALO + tm]
        bg = _dot(h_scr[HALO:HALO + tm, :], win_ref[...])
        y_ref[:, 2 * D_BRANCH:] = (bg * conv).astype(BF16)


def _mixer_in(x2d, layer, p, *, tm, seq_len):
    n_rows = x2d.shape[0]
    n_steps = len(_MIXER_IN_BLOCKS)

    def win_index(i, s):
        blk = jnp.int32(_MIXER_IN_BLOCKS[-1])
        for step in range(n_steps - 2, -1, -1):
            blk = jnp.where(s == step, _MIXER_IN_BLOCKS[step], blk)
        return (layer, 0, blk)

    prev_spec, next_spec = _halo_specs(tm, n_rows)
    const = lambda *shape: pl.BlockSpec((None,) + shape, lambda i, s: (layer,) + (0,) * len(shape))
    kernel = functools.partial(_mixer_in_kernel, tm=tm, tiles_per_seq=seq_len // tm, seq_len=seq_len)
    return pl.pallas_call(
        kernel,
        grid=(n_rows // tm, n_steps),
        in_specs=[
            pl.BlockSpec((tm, D_MODEL), lambda i, s: (i, 0)),
            prev_spec, next_spec,
            const(1, D_MODEL),
            pl.BlockSpec((None, D_MODEL, D_BRANCH), win_index),
            const(len(POOL_WINDOWS), POOL_GROUP, POOL_GROUP),
            const(1, D_BRANCH),
            const(1, D_BRANCH),
            const(N_SGU_GROUPS, SGU_CHUNK, SGU_CHUNK),
            const(N_SGU_GROUPS, SGU_CHUNK, SGU_GROUP),
            const(3, D_BRANCH),
        ],
        out_specs=[pl.BlockSpec((tm, 3 * D_BRANCH), lambda i, s: (i, 0)),
                   pl.BlockSpec((tm, D_MODEL), lambda i, s: (i, 0))],
        out_shape=[jax.ShapeDtypeStruct((n_rows, 3 * D_BRANCH), BF16),
                   jax.ShapeDtypeStruct((n_rows, D_MODEL), BF16)],
        scratch_shapes=[
            pltpu.VMEM((tm + 2 * HALO, D_MODEL), BF16),
            pltpu.VMEM((tm + 2 * HALO, D_BRANCH), F32),
            pltpu.VMEM((tm, D_BRANCH), BF16),
            pltpu.VMEM((tm, D_BRANCH), F32),
        ],
        compiler_params=pltpu.CompilerParams(
            dimension_semantics=("parallel", "arbitrary"), vmem_limit_bytes=V7X_VMEM_LIMIT_BYTES),
        name="mixer_in",
    )(x2d, x2d, x2d, p["norm1_g"], p["w_in"], p["w_pool"], p["pool_scale"], p["sgu_norm_g"],
      p["sgu_w"], p["sgu_b"], p["conv_w"])


def _mixer_out_kernel(h_ref, y_ref, wga_ref, wgb_ref, wgc_ref, wa_ref, wb_ref, wc_ref, wo_ref, out_ref,
                      *, bn_acc):
    s = pl.program_id(1)

    def step(first):
        for r0 in range(0, h_ref.shape[0], MXU_ACC_ROWS):
            rows = slice(r0, r0 + MXU_ACC_ROWS)
            h = h_ref[rows, :]
            merged = (jax.nn.sigmoid(_dot(h, wga_ref[...])) * _dot(y_ref[rows, 0:D_BRANCH], wa_ref[...])
                      + jax.nn.sigmoid(_dot(h, wgb_ref[...]))
                      * _dot(y_ref[rows, D_BRANCH:2 * D_BRANCH], wb_ref[...])
                      + jax.nn.sigmoid(_dot(h, wgc_ref[...])) * _dot(y_ref[rows, 2 * D_BRANCH:], wc_ref[...]))
            merged = merged.astype(BF16)
            for c in range(D_MODEL // bn_acc):
                cols = slice(c * bn_acc, (c + 1) * bn_acc)
                upd = _dot(merged, wo_ref[:, cols])
                if first:
                    out_ref[rows, cols] = upd
                else:
                    out_ref[rows, cols] += upd

    pl.when(s == 0)(functools.partial(step, True))
    pl.when(s > 0)(functools.partial(step, False))


def _mixer_out(h, y, layer, p, *, tm, bn):
    n_rows = h.shape[0]
    n_steps = D_MODEL // bn

    def gate_spec(branch):
        first = (OFF_GATE + branch * D_MODEL) // bn
        return pl.BlockSpec((None, D_MODEL, bn), lambda i, s: (layer, 0, first + s))

    branch_spec = pl.BlockSpec((None, D_BRANCH, bn), lambda i, s: (layer, 0, s))
    kernel = functools.partial(_mixer_out_kernel, bn_acc=512)
    return pl.pallas_call(
        kernel,
        grid=(n_rows // tm, n_steps),
        in_specs=[
            pl.BlockSpec((tm, D_MODEL), lambda i, s: (i, 0)),
            pl.BlockSpec((tm, 3 * D_BRANCH), lambda i, s: (i, 0)),
            gate_spec(0), gate_spec(1), gate_spec(2),
            branch_spec, branch_spec, branch_spec,
            pl.BlockSpec((None, bn, D_MODEL), lambda i, s: (layer, s, 0)),
        ],
        out_specs=pl.BlockSpec((tm, D_MODEL), lambda i, s: (i, 0)),
        out_shape=jax.ShapeDtypeStruct((n_rows, D_MODEL), F32),
        compiler_params=pltpu.CompilerParams(
            dimension_semantics=("parallel", "arbitrary"), vmem_limit_bytes=V7X_VMEM_LIMIT_BYTES),
        name="mixer_out",
    )(h, y, p["w_in"], p["w_in"], p["w_in"], p["w_pool_out"], p["w_sgu_out"], p["w_conv_out"], p["w_o"])


def _ffn_kernel(xm_ref, xp_ref, xn_ref, dm_ref, dp_ref, dn_ref, g2_ref, wg_ref, wv_ref, cwg_ref, cwv_ref,
                wd_ref, gf_ref, out_ref, h_scr, act0_scr, act1_scr, *, tm, tiles_per_seq, n_blocks, final_norm):
    i = pl.program_id(0)
    s = pl.program_id(1)
    ti = i % tiles_per_seq
    acts = (act0_scr, act1_scr)

    def conv3(up, cw_ref):
        return (cw_ref[0:1, :] * _shift_rows(up, -1) + cw_ref[1:2, :] * up
                + cw_ref[2:3, :] * _shift_rows(up, 1))[HALO:HALO + tm]

    def up_to(act_ref):
        h = h_scr[...]
        gate = conv3(_dot(h, wg_ref[...]), cwg_ref)
        val = conv3(_dot(h, wv_ref[...]), cwv_ref)
        act_ref[...] = (jax.nn.silu(gate) * val).astype(BF16)

    def down_from(act_ref):
        out_ref[...] += _dot(act_ref[...], wd_ref[...])

    @pl.when(s == 0)
    def _first():
        x_mid = xm_ref[...] + dm_ref[...]
        _norm_with_halo(h_scr, x_mid, xp_ref[...] + dp_ref[...], xn_ref[...] + dn_ref[...], g2_ref[...],
                        ti > 0, ti < tiles_per_seq - 1, tm)
        out_ref[...] = x_mid
        up_to(acts[0])

    for parity in (0, 1):
        @pl.when((s > 0) & (s < n_blocks) & (s % 2 == parity))
        def _middle(parity=parity):
            up_to(acts[parity])
            down_from(acts[1 - parity])

    @pl.when(s == n_blocks)
    def _last():
        down_from(acts[(n_blocks - 1) % 2])
        if final_norm:
            out_ref[...] = _rmsnorm(out_ref[...], gf_ref[...])


def _ffn(x2d, delta, layer, p, final_g, *, tm, bn, seq_len, final_norm):
    n_rows = x2d.shape[0]
    n_blocks = D_FF // bn
    prev_spec, next_spec = _halo_specs(tm, n_rows)
    main_spec = pl.BlockSpec((tm, D_MODEL), lambda i, s: (i, 0))
    up_blk = lambda s: jnp.minimum(s, n_blocks - 1)
    down_blk = lambda s: jnp.maximum(s - 1, 0)
    kernel = functools.partial(_ffn_kernel, tm=tm, tiles_per_seq=seq_len // tm, n_blocks=n_blocks,
                               final_norm=final_norm)
    return pl.pallas_call(
        kernel,
        grid=(n_rows // tm, n_blocks + 1),
        in_specs=[
            main_spec, prev_spec, next_spec,
            main_spec, prev_spec, next_spec,
            pl.BlockSpec((None, 1, D_MODEL), lambda i, s: (layer, 0, 0)),
            pl.BlockSpec((None, D_MODEL, bn), lambda i, s: (layer, 0, up_blk(s))),
            pl.BlockSpec((None, D_MODEL, bn), lambda i, s: (layer, 0, n_blocks + up_blk(s))),
            pl.BlockSpec((None, 3, bn), lambda i, s: (layer, 0, up_blk(s))),
            pl.BlockSpec((None, 3, bn), lambda i, s: (layer, 0, n_blocks + up_blk(s))),
            pl.BlockSpec((None, bn, D_MODEL), lambda i, s: (layer, down_blk(s), 0)),
            pl.BlockSpec((1, D_MODEL), lambda i, s: (0, 0)),
        ],
        out_specs=main_spec,
        out_shape=jax.ShapeDtypeStruct((n_rows, D_MODEL), F32),
        scratch_shapes=[pltpu.VMEM((tm + 2 * HALO, D_MODEL), BF16),
                        pltpu.VMEM((tm, bn), BF16), pltpu.VMEM((tm, bn), BF16)],
        compiler_params=pltpu.CompilerParams(
            dimension_semantics=("parallel", "arbitrary"), vmem_limit_bytes=V7X_VMEM_LIMIT_BYTES),
        name="ffn",
    )(x2d, x2d, x2d, delta, delta, delta, p["norm2_g"], p["w_up"], p["w_up"], p["ffn_conv_w"],
      p["ffn_conv_w"], p["w_down"], final_g)


def kernel(x, norm1_g, w_in, w_pool, pool_scale, sgu_norm_g, sgu_w, sgu_b, conv_w, w_pool_out, w_sgu_out,
           w_conv_out, w_o, norm2_g, w_up, ffn_conv_w, w_down, final_g):
    bsz, seq_len, d_model = x.shape
    depth = w_in.shape[0]
    assert d_model == D_MODEL and w_in.shape[2] == OFF_GATE + 3 * D_MODEL and w_up.shape[2] == 2 * D_FF
    tm_in, tm_out, tm_ffn = 512, 1024, 512
    assert all(seq_len % t == 0 and t % SGU_CHUNK == 0 for t in (tm_in, tm_out, tm_ffn))

    p = {
        "norm1_g": norm1_g[:, None, :], "norm2_g": norm2_g[:, None, :],
        "pool_scale": pool_scale[:, None, :], "sgu_norm_g": sgu_norm_g[:, None, :],
        "w_in": w_in.astype(BF16), "w_pool": w_pool.astype(BF16), "sgu_w": sgu_w.astype(BF16),
        "sgu_b": jnp.broadcast_to(jnp.transpose(sgu_b, (0, 2, 1))[..., None],
                                  (depth, N_SGU_GROUPS, SGU_CHUNK, SGU_GROUP)),
        "conv_w": conv_w, "ffn_conv_w": ffn_conv_w,
        "w_pool_out": w_pool_out.astype(BF16), "w_sgu_out": w_sgu_out.astype(BF16),
        "w_conv_out": w_conv_out.astype(BF16), "w_o": w_o.astype(BF16),
        "w_up": w_up.astype(BF16), "w_down": w_down.astype(BF16),
    }
    x2d = x.reshape(bsz * seq_len, d_model)
    for layer in range(depth):
        y, h = _mixer_in(x2d, layer, p, tm=tm_in, seq_len=seq_len)
        delta = _mixer_out(h, y, layer, p, tm=tm_out, bn=256)
        x2d = _ffn(x2d, delta, layer, p, final_g[None, :], tm=tm_ffn, bn=512, seq_len=seq_len,
                   final_norm=(layer == depth - 1))
    return x2d.reshape(bsz, seq_len, d_model)
```

```python
import functools

import jax
import jax.numpy as jnp
from jax import lax
from jax.experimental import pallas as pl
from jax.experimental.pallas import tpu as pltpu

D_MODEL = 2048
D_BRANCH = 1024
POOL_WINDOWS = (2, 4, 8, 16)
POOL_GROUP = D_BRANCH // len(POOL_WINDOWS)
SGU_CHUNK = 128
N_SGU_GROUPS = 8
SGU_GROUP = D_BRANCH // N_SGU_GROUPS
D_FF = 5632
EPS = 1e-6
OFF_GATE = 6 * D_BRANCH

HALO = 16
MXU_ACC_ROWS = 512
V7X_VMEM_LIMIT_BYTES = 58 * 1024 * 1024

F32 = jnp.float32
BF16 = jnp.bfloat16


def _rmsnorm(xf, g):
    return xf * lax.rsqrt(jnp.mean(xf * xf, axis=-1, keepdims=True) + EPS) * g


def _dot(a, b):
    return jnp.dot(a, b, preferred_element_type=F32)


def _shift_rows(x, k):
    n = x.shape[0]
    return pltpu.roll(x, (-k) % n, axis=0)


def _norm_with_halo(h_scr, x_main, x_prev, x_next, g, has_prev, has_next, tm):
    h_scr[0:HALO, :] = jnp.where(has_prev, _rmsnorm(x_prev, g), 0.0).astype(BF16)
    h_scr[HALO:HALO + tm, :] = _rmsnorm(x_main, g).astype(BF16)
    h_scr[HALO + tm:, :] = jnp.where(has_next, _rmsnorm(x_next, g), 0.0).astype(BF16)


def _halo_specs(tm, n_rows):
    blocks_per_tile = tm // HALO
    last = n_rows // HALO - 1
    prev = pl.BlockSpec((HALO, D_MODEL), lambda i, s: (jnp.maximum(i * blocks_per_tile - 1, 0), 0))
    nxt = pl.BlockSpec((HALO, D_MODEL), lambda i, s: (jnp.minimum((i + 1) * blocks_per_tile, last), 0))
    return prev, nxt


_MIXER_IN_BLOCKS = (0, 2, 1, 3, 5, 4)


def _mixer_in_kernel(xm_ref, xp_ref, xn_ref, g1_ref, win_ref, wpool_ref, pscale_ref, sgug_ref,
                     sguw_ref, sgub_ref, convw_ref, y_ref, h_ref, h_scr, cx_scr, vn_scr, z_scr,
                     *, tm, tiles_per_seq, seq_len):
    i = pl.program_id(0)
    s = pl.program_id(1)
    ti = i % tiles_per_seq

    @pl.when(s == 0)
    def _pool():
        _norm_with_halo(h_scr, xm_ref[...], xp_ref[...], xn_ref[...], g1_ref[...],
                        ti > 0, ti < tiles_per_seq - 1, tm)
        h_ref[...] = h_scr[HALO:HALO + tm, :]
        a = _dot(h_scr[...], win_ref[...])
        pos = ti * tm + lax.broadcasted_iota(jnp.int32, (tm, POOL_GROUP), 0)
        for gi, w in enumerate(POOL_WINDOWS):
            cols = slice(gi * POOL_GROUP, (gi + 1) * POOL_GROUP)
            xg = a[:, cols]
            q, m = xg, 1
            while m < w // 2:
                q = q + _shift_rows(q, m)
                m *= 2
            win_sum = (q + _shift_rows(q, -(w // 2)))[HALO:HALO + tm]
            lo = jnp.maximum(pos - w // 2, 0)
            hi = jnp.minimum(pos + (w - w // 2 - 1), seq_len - 1)
            cnt = (hi - lo + 1).astype(F32)
            pooled = win_sum / cnt - xg[HALO:HALO + tm]
            y = _dot(pooled.astype(BF16), wpool_ref[gi]) * pscale_ref[:, cols]
            y_ref[:, cols] = y.astype(BF16)

    @pl.when(s == 1)
    def _sgu_v():
        gv = jax.nn.gelu(_dot(h_scr[HALO:HALO + tm, :], win_ref[...]))
        vn_scr[...] = _rmsnorm(gv, sgug_ref[...]).astype(BF16)
        n_chunk = tm // SGU_CHUNK
        for g in range(N_SGU_GROUPS):
            cols = slice(g * SGU_GROUP, (g + 1) * SGU_GROUP)
            rhs = jnp.concatenate(
                [vn_scr[c * SGU_CHUNK:(c + 1) * SGU_CHUNK, cols] for c in range(n_chunk)], axis=1)
            o = _dot(sguw_ref[g], rhs)
            for c in range(n_chunk):
                z_scr[c * SGU_CHUNK:(c + 1) * SGU_CHUNK, cols] = (
                    o[:, c * SGU_GROUP:(c + 1) * SGU_GROUP] + sgub_ref[g])

    @pl.when(s == 2)
    def _sgu_u():
        u = jax.nn.gelu(_dot(h_scr[HALO:HALO + tm, :], win_ref[...]))
        y_ref[:, D_BRANCH:2 * D_BRANCH] = (u * z_scr[...]).astype(BF16)

    @pl.when(s == 3)
    def _conv_x():
        cx_scr[...] = _dot(h_scr[...], win_ref[...])

    @pl.when(s == 4)
    def _conv_c():
        cx_scr[...] = cx_scr[...] * _dot(h_scr[...], win_ref[...])

    @pl.when(s == 5)
    def _conv_b():
        cx = cx_scr[...]
        conv = (convw_ref[0:1, :] * _shift_rows(cx, -1) + convw_ref[1:2, :] * cx
                + convw_ref[2:3, :] * _shift_rows(cx, 1))[HALO:HALO + tm]
        bg = _dot(h_scr[HALO:HALO + tm, :], win_ref[...])
        y_ref[:, 2 * D_BRANCH:] = (bg * conv).astype(BF16)


def _mixer_in(x2d, layer, p, *, tm, seq_len):
    n_rows = x2d.shape[0]
    n_steps = len(_MIXER_IN_BLOCKS)

    def win_index(i, s):
        blk = jnp.int32(_MIXER_IN_BLOCKS[-1])
        for step in range(n_steps - 2, -1, -1):
            blk = jnp.where(s == step, _MIXER_IN_BLOCKS[step], blk)
        return (layer, 0, blk)

    prev_spec, next_spec = _halo_specs(tm, n_rows)
    const = lambda *shape: pl.BlockSpec((None,) + shape, lambda i, s: (layer,) + (0,) * len(shape))
    kernel = functools.partial(_mixer_in_kernel, tm=tm, tiles_per_seq=seq_len // tm, seq_len=seq_len)
    return pl.pallas_call(
        kernel,
        grid=(n_rows // tm, n_steps),
        in_specs=[
            pl.BlockSpec((tm, D_MODEL), lambda i, s: (i, 0)),
            prev_spec, next_spec,
            const(1, D_MODEL),
            pl.BlockSpec((None, D_MODEL, D_BRANCH), win_index),
            const(len(POOL_WINDOWS), POOL_GROUP, POOL_GROUP),
            const(1, D_BRANCH),
            const(1, D_BRANCH),
            const(N_SGU_GROUPS, SGU_CHUNK, SGU_CHUNK),
            const(N_SGU_GROUPS, SGU_CHUNK, SGU_GROUP),
            const(3, D_BRANCH),
        ],
        out_specs=[pl.BlockSpec((tm, 3 * D_BRANCH), lambda i, s: (i, 0)),
                   pl.BlockSpec((tm, D_MODEL), lambda i, s: (i, 0))],
        out_shape=[jax.ShapeDtypeStruct((n_rows, 3 * D_BRANCH), BF16),
                   jax.ShapeDtypeStruct((n_rows, D_MODEL), BF16)],
        scratch_shapes=[
            pltpu.VMEM((tm + 2 * HALO, D_MODEL), BF16),
            pltpu.VMEM((tm + 2 * HALO, D_BRANCH), F32),
            pltpu.VMEM((tm, D_BRANCH), BF16),
            pltpu.VMEM((tm, D_BRANCH), F32),
        ],
        compiler_params=pltpu.CompilerParams(
            dimension_semantics=("parallel", "arbitrary"), vmem_limit_bytes=V7X_VMEM_LIMIT_BYTES),
        name="mixer_in",
    )(x2d, x2d, x2d, p["norm1_g"], p["w_in"], p["w_pool"], p["pool_scale"], p["sgu_norm_g"],
      p["sgu_w"], p["sgu_b"], p["conv_w"])


def _mixer_out_kernel(h_ref, y_ref, wga_ref, wgb_ref, wgc_ref, wa_ref, wb_ref, wc_ref, wo_ref, out_ref,
                      *, bn_acc):
    s = pl.program_id(1)

    def step(first):
        for r0 in range(0, h_ref.shape[0], MXU_ACC_ROWS):
            rows = slice(r0, r0 + MXU_ACC_ROWS)
            h = h_ref[rows, :]
            merged = (jax.nn.sigmoid(_dot(h, wga_ref[...])) * _dot(y_ref[rows, 0:D_BRANCH], wa_ref[...])
                      + jax.nn.sigmoid(_dot(h, wgb_ref[...]))
                      * _dot(y_ref[rows, D_BRANCH:2 * D_BRANCH], wb_ref[...])
                      + jax.nn.sigmoid(_dot(h, wgc_ref[...])) * _dot(y_ref[rows, 2 * D_BRANCH:], wc_ref[...]))
            merged = merged.astype(BF16)
            for c in range(D_MODEL // bn_acc):
                cols = slice(c * bn_acc, (c + 1) * bn_acc)
                upd = _dot(merged, wo_ref[:, cols])
                if first:
                    out_ref[rows, cols] = upd
                else:
                    out_ref[rows, cols] += upd

    pl.when(s == 0)(functools.partial(step, True))
    pl.when(s > 0)(functools.partial(step, False))


def _mixer_out(h, y, layer, p, *, tm, bn):
    n_rows = h.shape[0]
    n_steps = D_MODEL // bn

    def gate_spec(branch):
        first = (OFF_GATE + branch * D_MODEL) // bn
        return pl.BlockSpec((None, D_MODEL, bn), lambda i, s: (layer, 0, first + s))

    branch_spec = pl.BlockSpec((None, D_BRANCH, bn), lambda i, s: (layer, 0, s))
    kernel = functools.partial(_mixer_out_kernel, bn_acc=512)
    return pl.pallas_call(
        kernel,
        grid=(n_rows // tm, n_steps),
        in_specs=[
            pl.BlockSpec((tm, D_MODEL), lambda i, s: (i, 0)),
            pl.BlockSpec((tm, 3 * D_BRANCH), lambda i, s: (i, 0)),
            gate_spec(0), gate_spec(1), gate_spec(2),
            branch_spec, branch_spec, branch_spec,
            pl.BlockSpec((None, bn, D_MODEL), lambda i, s: (layer, s, 0)),
        ],
        out_specs=pl.BlockSpec((tm, D_MODEL), lambda i, s: (i, 0)),
        out_shape=jax.ShapeDtypeStruct((n_rows, D_MODEL), F32),
        compiler_params=pltpu.CompilerParams(
            dimension_semantics=("parallel", "arbitrary"), vmem_limit_bytes=V7X_VMEM_LIMIT_BYTES),
        name="mixer_out",
    )(h, y, p["w_in"], p["w_in"], p["w_in"], p["w_pool_out"], p["w_sgu_out"], p["w_conv_out"], p["w_o"])


def _ffn_kernel(xm_ref, dm_ref, xe_ref, de_ref, g2_ref, wg_ref, wv_ref, cwg_ref, cwv_ref, wd_ref, gf_ref,
                out_ref, h_scr, he_scr, eg_scr, ev_scr, *, tm, n_tiles, tiles_per_seq, n_blocks, final_norm):
    i = pl.program_id(0)
    s = pl.program_id(1)
    ti = i % tiles_per_seq

    @pl.when(s == 0)
    def _init():
        x_mid = xm_ref[...] + dm_ref[...]
        h_scr[...] = _rmsnorm(x_mid, g2_ref[...]).astype(BF16)
        out_ref[...] = x_mid

    @pl.when((s == 0) & (i == 0))
    def _edge_norm():
        he_scr[...] = _rmsnorm(xe_ref[...] + de_ref[...], g2_ref[...]).astype(BF16)

    @pl.when(i == 0)
    def _edge_up():
        eg_scr[s] = _dot(he_scr[...], wg_ref[...])
        ev_scr[s] = _dot(he_scr[...], wv_ref[...])

    row = lax.broadcasted_iota(jnp.int32, (tm, 1), 0)

    def conv3(up, edge_scr, cw_ref):
        before = jnp.where(ti > 0, edge_scr[s, pl.ds(i, 1), :], 0.0)
        after = jnp.where(ti < tiles_per_seq - 1, edge_scr[s, pl.ds(n_tiles + i, 1), :], 0.0)
        prev = jnp.where(row == 0, before, _shift_rows(up, -1))
        nxt = jnp.where(row == tm - 1, after, _shift_rows(up, 1))
        return cw_ref[0:1, :] * prev + cw_ref[1:2, :] * up + cw_ref[2:3, :] * nxt

    h = h_scr[...]
    gate = conv3(_dot(h, wg_ref[...]), eg_scr, cwg_ref)
    val = conv3(_dot(h, wv_ref[...]), ev_scr, cwv_ref)
    act = (jax.nn.silu(gate) * val).astype(BF16)
    out_ref[...] += _dot(act, wd_ref[...])

    if final_norm:
        @pl.when(s == n_blocks - 1)
        def _final():
            out_ref[...] = _rmsnorm(out_ref[...], gf_ref[...])


def _ffn(x2d, delta, layer, p, final_g, *, tm, bn, seq_len, final_norm):
    n_rows = x2d.shape[0]
    n_tiles = n_rows // tm
    n_blocks = D_FF // bn
    starts = [t * tm for t in range(n_tiles)]
    edge_rows = jnp.asarray([max(r - 1, 0) for r in starts] + [min(r + tm, n_rows - 1) for r in starts],
                            dtype=jnp.int32)
    x_edge = jnp.take(x2d, edge_rows, axis=0)
    d_edge = jnp.take(delta, edge_rows, axis=0)
    n_edge = 2 * n_tiles

    main_spec = pl.BlockSpec((tm, D_MODEL), lambda i, s: (i, 0))
    edge_spec = pl.BlockSpec((n_edge, D_MODEL), lambda i, s: (0, 0))
    kernel = functools.partial(_ffn_kernel, tm=tm, n_tiles=n_tiles, tiles_per_seq=seq_len // tm,
                               n_blocks=n_blocks, final_norm=final_norm)
    return pl.pallas_call(
        kernel,
        grid=(n_tiles, n_blocks),
        in_specs=[
            main_spec, main_spec,
            edge_spec, edge_spec,
            pl.BlockSpec((None, 1, D_MODEL), lambda i, s: (layer, 0, 0)),
            pl.BlockSpec((None, D_MODEL, bn), lambda i, s: (layer, 0, s)),
            pl.BlockSpec((None, D_MODEL, bn), lambda i, s: (layer, 0, n_blocks + s)),
            pl.BlockSpec((None, 3, bn), lambda i, s: (layer, 0, s)),
            pl.BlockSpec((None, 3, bn), lambda i, s: (layer, 0, n_blocks + s)),
            pl.BlockSpec((None, bn, D_MODEL), lambda i, s: (layer, s, 0)),
            pl.BlockSpec((1, D_MODEL), lambda i, s: (0, 0)),
        ],
        out_specs=main_spec,
        out_shape=jax.ShapeDtypeStruct((n_rows, D_MODEL), F32),
        scratch_shapes=[pltpu.VMEM((tm, D_MODEL), BF16),
                        pltpu.VMEM((n_edge, D_MODEL), BF16),
                        pltpu.VMEM((n_blocks, n_edge, bn), F32),
                        pltpu.VMEM((n_blocks, n_edge, bn), F32)],
        compiler_params=pltpu.CompilerParams(
            dimension_semantics=("arbitrary", "arbitrary"), vmem_limit_bytes=V7X_VMEM_LIMIT_BYTES),
        name="ffn",
    )(x2d, delta, x_edge, d_edge, p["norm2_g"], p["w_up"], p["w_up"], p["ffn_conv_w"], p["ffn_conv_w"],
      p["w_down"], final_g)


def kernel(x, norm1_g, w_in, w_pool, pool_scale, sgu_norm_g, sgu_w, sgu_b, conv_w, w_pool_out, w_sgu_out,
           w_conv_out, w_o, norm2_g, w_up, ffn_conv_w, w_down, final_g):
    bsz, seq_len, d_model = x.shape
    depth = w_in.shape[0]
    assert d_model == D_MODEL and w_in.shape[2] == OFF_GATE + 3 * D_MODEL and w_up.shape[2] == 2 * D_FF
    tm_in, tm_out, tm_ffn = 512, 1024, 512
    assert all(seq_len % t == 0 and t % SGU_CHUNK == 0 for t in (tm_in, tm_out, tm_ffn))

    p = {
        "norm1_g": norm1_g[:, None, :], "norm2_g": norm2_g[:, None, :],
        "pool_scale": pool_scale[:, None, :], "sgu_norm_g": sgu_norm_g[:, None, :],
        "w_in": w_in.astype(BF16), "w_pool": w_pool.astype(BF16), "sgu_w": sgu_w.astype(BF16),
        "sgu_b": jnp.broadcast_to(jnp.transpose(sgu_b, (0, 2, 1))[..., None],
                                  (depth, N_SGU_GROUPS, SGU_CHUNK, SGU_GROUP)),
        "conv_w": conv_w, "ffn_conv_w": ffn_conv_w,
        "w_pool_out": w_pool_out.astype(BF16), "w_sgu_out": w_sgu_out.astype(BF16),
        "w_conv_out": w_conv_out.astype(BF16), "w_o": w_o.astype(BF16),
        "w_up": w_up.astype(BF16), "w_down": w_down.astype(BF16),
    }
    x2d = x.reshape(bsz * seq_len, d_model)
    for layer in range(depth):
        y, h = _mixer_in(x2d, layer, p, tm=tm_in, seq_len=seq_len)
        delta = _mixer_out(h, y, layer, p, tm=tm_out, bn=256)
        x2d = _ffn(x2d, delta, layer, p, final_g[None, :], tm=tm_ffn, bn=512, seq_len=seq_len,
                   final_norm=(layer == depth - 1))
    return x2d.reshape(bsz, seq_len, d_model)
```

```python
import functools

import jax
import jax.numpy as jnp
from jax import lax
from jax.experimental import pallas as pl
from jax.experimental.pallas import tpu as pltpu

D_MODEL = 2048
D_BRANCH = 1024
POOL_WINDOWS = (2, 4, 8, 16)
POOL_GROUP = D_BRANCH // len(POOL_WINDOWS)
SGU_CHUNK = 128
N_SGU_GROUPS = 8
SGU_GROUP = D_BRANCH // N_SGU_GROUPS
D_FF = 5632
EPS = 1e-6
OFF_GATE = 6 * D_BRANCH

HALO = 16
SUBLANES = 8
LANES = 128
MXU_ACC_ROWS = 512
V7X_VMEM_LIMIT_BYTES = 58 * 1024 * 1024

F32 = jnp.float32
BF16 = jnp.bfloat16


def _rmsnorm(xf, g):
    return xf * lax.rsqrt(jnp.mean(xf * xf, axis=-1, keepdims=True) + EPS) * g


def _dot(a, b):
    return jnp.dot(a, b, preferred_element_type=F32)


def _shift_rows(x, k):
    n = x.shape[0]
    return pltpu.roll(x, (-k) % n, axis=0)


def _norm_with_halo(h_scr, x_main, x_prev, x_next, g, has_prev, has_next, tm):
    h_scr[0:HALO, :] = jnp.where(has_prev, _rmsnorm(x_prev, g), 0.0).astype(BF16)
    h_scr[HALO:HALO + tm, :] = _rmsnorm(x_main, g).astype(BF16)
    h_scr[HALO + tm:, :] = jnp.where(has_next, _rmsnorm(x_next, g), 0.0).astype(BF16)


def _halo_specs(tm, n_rows):
    blocks_per_tile = tm // HALO
    last = n_rows // HALO - 1
    prev = pl.BlockSpec((HALO, D_MODEL), lambda i, s: (jnp.maximum(i * blocks_per_tile - 1, 0), 0))
    nxt = pl.BlockSpec((HALO, D_MODEL), lambda i, s: (jnp.minimum((i + 1) * blocks_per_tile, last), 0))
    return prev, nxt


_MIXER_IN_BLOCKS = (0, 2, 1, 3, 5, 4)


def _mixer_in_kernel(xm_ref, xp_ref, xn_ref, g1_ref, win_ref, wpool_ref, pscale_ref, sgug_ref,
                     sguw_ref, sgub_ref, convw_ref, y_ref, h_ref, h_scr, cx_scr, vn_scr, z_scr,
                     *, tm, tiles_per_seq, seq_len):
    i = pl.program_id(0)
    s = pl.program_id(1)
    ti = i % tiles_per_seq

    @pl.when(s == 0)
    def _pool():
        _norm_with_halo(h_scr, xm_ref[...], xp_ref[...], xn_ref[...], g1_ref[...],
                        ti > 0, ti < tiles_per_seq - 1, tm)
        h_ref[...] = h_scr[HALO:HALO + tm, :]
        a = _dot(h_scr[...], win_ref[...])
        pos = ti * tm + lax.broadcasted_iota(jnp.int32, (tm, POOL_GROUP), 0)
        for gi, w in enumerate(POOL_WINDOWS):
            cols = slice(gi * POOL_GROUP, (gi + 1) * POOL_GROUP)
            xg = a[:, cols]
            q, m = xg, 1
            while m < w // 2:
                q = q + _shift_rows(q, m)
                m *= 2
            win_sum = (q + _shift_rows(q, -(w // 2)))[HALO:HALO + tm]
            lo = jnp.maximum(pos - w // 2, 0)
            hi = jnp.minimum(pos + (w - w // 2 - 1), seq_len - 1)
            cnt = (hi - lo + 1).astype(F32)
            pooled = win_sum / cnt - xg[HALO:HALO + tm]
            y = _dot(pooled.astype(BF16), wpool_ref[gi]) * pscale_ref[:, cols]
            y_ref[:, cols] = y.astype(BF16)

    @pl.when(s == 1)
    def _sgu_v():
        gv = jax.nn.gelu(_dot(h_scr[HALO:HALO + tm, :], win_ref[...]))
        vn_scr[...] = _rmsnorm(gv, sgug_ref[...]).astype(BF16)
        n_chunk = tm // SGU_CHUNK
        for g in range(N_SGU_GROUPS):
            cols = slice(g * SGU_GROUP, (g + 1) * SGU_GROUP)
            rhs = jnp.concatenate(
                [vn_scr[c * SGU_CHUNK:(c + 1) * SGU_CHUNK, cols] for c in range(n_chunk)], axis=1)
            o = _dot(sguw_ref[g], rhs)
            for c in range(n_chunk):
                z_scr[c * SGU_CHUNK:(c + 1) * SGU_CHUNK, cols] = (
                    o[:, c * SGU_GROUP:(c + 1) * SGU_GROUP] + sgub_ref[g])

    @pl.when(s == 2)
    def _sgu_u():
        u = jax.nn.gelu(_dot(h_scr[HALO:HALO + tm, :], win_ref[...]))
        y_ref[:, D_BRANCH:2 * D_BRANCH] = (u * z_scr[...]).astype(BF16)

    @pl.when(s == 3)
    def _conv_x():
        cx_scr[...] = _dot(h_scr[...], win_ref[...])

    @pl.when(s == 4)
    def _conv_c():
        cx_scr[...] = cx_scr[...] * _dot(h_scr[...], win_ref[...])

    @pl.when(s == 5)
    def _conv_b():
        cx = cx_scr[...]
        conv = (convw_ref[0:1, :] * _shift_rows(cx, -1) + convw_ref[1:2, :] * cx
                + convw_ref[2:3, :] * _shift_rows(cx, 1))[HALO:HALO + tm]
        bg = _dot(h_scr[HALO:HALO + tm, :], win_ref[...])
        y_ref[:, 2 * D_BRANCH:] = (bg * conv).astype(BF16)


def _mixer_in(x2d, layer, p, *, tm, seq_len):
    n_rows = x2d.shape[0]
    n_steps = len(_MIXER_IN_BLOCKS)

    def win_index(i, s):
        blk = jnp.int32(_MIXER_IN_BLOCKS[-1])
        for step in range(n_steps - 2, -1, -1):
            blk = jnp.where(s == step, _MIXER_IN_BLOCKS[step], blk)
        return (layer, 0, blk)

    prev_spec, next_spec = _halo_specs(tm, n_rows)
    const = lambda *shape: pl.BlockSpec((None,) + shape, lambda i, s: (layer,) + (0,) * len(shape))
    kernel = functools.partial(_mixer_in_kernel, tm=tm, tiles_per_seq=seq_len // tm, seq_len=seq_len)
    return pl.pallas_call(
        kernel,
        grid=(n_rows // tm, n_steps),
        in_specs=[
            pl.BlockSpec((tm, D_MODEL), lambda i, s: (i, 0)),
            prev_spec, next_spec,
            const(1, D_MODEL),
            pl.BlockSpec((None, D_MODEL, D_BRANCH), win_index),
            const(len(POOL_WINDOWS), POOL_GROUP, POOL_GROUP),
            const(1, D_BRANCH),
            const(1, D_BRANCH),
            const(N_SGU_GROUPS, SGU_CHUNK, SGU_CHUNK),
            const(N_SGU_GROUPS, SGU_CHUNK, SGU_GROUP),
            const(3, D_BRANCH),
        ],
        out_specs=[pl.BlockSpec((tm, 3 * D_BRANCH), lambda i, s: (i, 0)),
                   pl.BlockSpec((tm, D_MODEL), lambda i, s: (i, 0))],
        out_shape=[jax.ShapeDtypeStruct((n_rows, 3 * D_BRANCH), BF16),
                   jax.ShapeDtypeStruct((n_rows, D_MODEL), BF16)],
        scratch_shapes=[
            pltpu.VMEM((tm + 2 * HALO, D_MODEL), BF16),
            pltpu.VMEM((tm + 2 * HALO, D_BRANCH), F32),
            pltpu.VMEM((tm, D_BRANCH), BF16),
            pltpu.VMEM((tm, D_BRANCH), F32),
        ],
        compiler_params=pltpu.CompilerParams(
            dimension_semantics=("parallel", "arbitrary"), vmem_limit_bytes=V7X_VMEM_LIMIT_BYTES),
        name="mixer_in",
    )(x2d, x2d, x2d, p["norm1_g"], p["w_in"], p["w_pool"], p["pool_scale"], p["sgu_norm_g"],
      p["sgu_w"], p["sgu_b"], p["conv_w"])


def _mixer_out_kernel(h_ref, y_ref, wga_ref, wgb_ref, wgc_ref, wa_ref, wb_ref, wc_ref, wo_ref, out_ref,
                      *, bn_acc):
    s = pl.program_id(1)

    def step(first):
        for r0 in range(0, h_ref.shape[0], MXU_ACC_ROWS):
            rows = slice(r0, r0 + MXU_ACC_ROWS)
            h = h_ref[rows, :]
            merged = (jax.nn.sigmoid(_dot(h, wga_ref[...])) * _dot(y_ref[rows, 0:D_BRANCH], wa_ref[...])
                      + jax.nn.sigmoid(_dot(h, wgb_ref[...]))
                      * _dot(y_ref[rows, D_BRANCH:2 * D_BRANCH], wb_ref[...])
                      + jax.nn.sigmoid(_dot(h, wgc_ref[...])) * _dot(y_ref[rows, 2 * D_BRANCH:], wc_ref[...]))
            merged = merged.astype(BF16)
            for c in range(D_MODEL // bn_acc):
                cols = slice(c * bn_acc, (c + 1) * bn_acc)
                upd = _dot(merged, wo_ref[:, cols])
                if first:
                    out_ref[rows, cols] = upd
                else:
                    out_ref[rows, cols] += upd

    pl.when(s == 0)(functools.partial(step, True))
    pl.when(s > 0)(functools.partial(step, False))


def _mixer_out(h, y, layer, p, *, tm, bn):
    n_rows = h.shape[0]
    n_steps = D_MODEL // bn

    def gate_spec(branch):
        first = (OFF_GATE + branch * D_MODEL) // bn
        return pl.BlockSpec((None, D_MODEL, bn), lambda i, s: (layer, 0, first + s))

    branch_spec = pl.BlockSpec((None, D_BRANCH, bn), lambda i, s: (layer, 0, s))
    kernel = functools.partial(_mixer_out_kernel, bn_acc=512)
    return pl.pallas_call(
        kernel,
        grid=(n_rows // tm, n_steps),
        in_specs=[
            pl.BlockSpec((tm, D_MODEL), lambda i, s: (i, 0)),
            pl.BlockSpec((tm, 3 * D_BRANCH), lambda i, s: (i, 0)),
            gate_spec(0), gate_spec(1), gate_spec(2),
            branch_spec, branch_spec, branch_spec,
            pl.BlockSpec((None, bn, D_MODEL), lambda i, s: (layer, s, 0)),
        ],
        out_specs=pl.BlockSpec((tm, D_MODEL), lambda i, s: (i, 0)),
        out_shape=jax.ShapeDtypeStruct((n_rows, D_MODEL), F32),
        compiler_params=pltpu.CompilerParams(
            dimension_semantics=("parallel", "arbitrary"), vmem_limit_bytes=V7X_VMEM_LIMIT_BYTES),
        name="mixer_out",
    )(h, y, p["w_in"], p["w_in"], p["w_in"], p["w_pool_out"], p["w_sgu_out"], p["w_conv_out"], p["w_o"])


def _ffn_kernel(xm_ref, dm_ref, xe_ref, de_ref, g2_ref, wg_ref, wv_ref, cwg_ref, cwv_ref, wd_ref, gf_ref,
                out_ref, h_scr, acc_scr, cols_scr, he_scr, eg_scr, ev_scr,
                *, tm, n_tiles, tiles_per_seq, n_blocks, final_norm):
    i = pl.program_id(0)
    s = pl.program_id(1)
    ti = i % tiles_per_seq
    n_slabs = tm // SUBLANES

    n_lane_tiles = D_MODEL // LANES
    run_pitch = n_slabs + SUBLANES

    def strided_rows(start, stride):
        return jnp.concatenate(
            [cols_scr[j, pl.ds(start, SUBLANES, stride=stride), :] for j in range(n_lane_tiles)], axis=1)

    @pl.when(s == 0)
    def _init():
        for j in range(n_lane_tiles):
            cols = slice(j * LANES, (j + 1) * LANES)
            for a in range(SUBLANES):
                src = slice(a * n_slabs, (a + 1) * n_slabs)
                cols_scr[j, a * run_pitch:a * run_pitch + n_slabs, :] = xm_ref[src, cols] + dm_ref[src, cols]

        def slab(b, carry):
            acc_scr[pl.ds(pl.multiple_of(SUBLANES * b, SUBLANES), SUBLANES), :] = strided_rows(b, run_pitch)
            return carry
        lax.fori_loop(0, n_slabs, slab, 0, unroll=8)
        h_scr[...] = _rmsnorm(acc_scr[...], g2_ref[...]).astype(BF16)

    @pl.when((s == 0) & (i == 0))
    def _edge_norm():
        he_scr[...] = _rmsnorm(xe_ref[...] + de_ref[...], g2_ref[...]).astype(BF16)

    @pl.when(i == 0)
    def _edge_up():
        eg_scr[s] = _dot(he_scr[...], wg_ref[...])
        ev_scr[s] = _dot(he_scr[...], wv_ref[...])

    sub = lax.broadcasted_iota(jnp.int32, (SUBLANES, 1), 0)

    def conv3(up, edge_scr, cw_ref):
        before = jnp.where(ti > 0, edge_scr[s, pl.ds(i, 1), :], 0.0)
        after = jnp.where(ti < tiles_per_seq - 1, edge_scr[s, pl.ds(n_tiles + i, 1), :], 0.0)
        prev0 = jnp.where(sub == 0, before, pltpu.roll(up[tm - SUBLANES:, :], 1, axis=0))
        next_last = jnp.where(sub == SUBLANES - 1, after, pltpu.roll(up[:SUBLANES, :], SUBLANES - 1, axis=0))
        prev = jnp.concatenate([prev0, up[:tm - SUBLANES, :]], axis=0)
        nxt = jnp.concatenate([up[SUBLANES:, :], next_last], axis=0)
        return cw_ref[0:1, :] * prev + cw_ref[1:2, :] * up + cw_ref[2:3, :] * nxt

    h = h_scr[...]
    gate = conv3(_dot(h, wg_ref[...]), eg_scr, cwg_ref)
    val = conv3(_dot(h, wv_ref[...]), ev_scr, cwv_ref)
    act = (jax.nn.silu(gate) * val).astype(BF16)
    acc_scr[...] += _dot(act, wd_ref[...])

    @pl.when(s == n_blocks - 1)
    def _store():
        for j in range(n_lane_tiles):
            cols_scr[j, 0:tm, :] = acc_scr[:, j * LANES:(j + 1) * LANES]

        def slab(m, carry):
            per = n_slabs // SUBLANES
            rows = strided_rows(SUBLANES * SUBLANES * (m % per) + m // per, SUBLANES)
            if final_norm:
                rows = _rmsnorm(rows, gf_ref[...])
            out_ref[pl.ds(pl.multiple_of(SUBLANES * m, SUBLANES), SUBLANES), :] = rows
            return carry
        lax.fori_loop(0, n_slabs, slab, 0, unroll=8)


def _ffn(x2d, delta, layer, p, final_g, *, tm, bn, seq_len, final_norm):
    n_rows = x2d.shape[0]
    n_tiles = n_rows // tm
    n_blocks = D_FF // bn
    starts = [t * tm for t in range(n_tiles)]
    edge_rows = jnp.asarray([max(r - 1, 0) for r in starts] + [min(r + tm, n_rows - 1) for r in starts],
                            dtype=jnp.int32)
    x_edge = jnp.take(x2d, edge_rows, axis=0)
    d_edge = jnp.take(delta, edge_rows, axis=0)
    n_edge = 2 * n_tiles

    main_spec = pl.BlockSpec((tm, D_MODEL), lambda i, s: (i, 0))
    edge_spec = pl.BlockSpec((n_edge, D_MODEL), lambda i, s: (0, 0))
    kernel = functools.partial(_ffn_kernel, tm=tm, n_tiles=n_tiles, tiles_per_seq=seq_len // tm,
                               n_blocks=n_blocks, final_norm=final_norm)
    return pl.pallas_call(
        kernel,
        grid=(n_tiles, n_blocks),
        in_specs=[
            main_spec, main_spec,
            edge_spec, edge_spec,
            pl.BlockSpec((None, 1, D_MODEL), lambda i, s: (layer, 0, 0)),
            pl.BlockSpec((None, D_MODEL, bn), lambda i, s: (layer, 0, s)),
            pl.BlockSpec((None, D_MODEL, bn), lambda i, s: (layer, 0, n_blocks + s)),
            pl.BlockSpec((None, 3, bn), lambda i, s: (layer, 0, s)),
            pl.BlockSpec((None, 3, bn), lambda i, s: (layer, 0, n_blocks + s)),
            pl.BlockSpec((None, bn, D_MODEL), lambda i, s: (layer, s, 0)),
            pl.BlockSpec((1, D_MODEL), lambda i, s: (0, 0)),
        ],
        out_specs=main_spec,
        out_shape=jax.ShapeDtypeStruct((n_rows, D_MODEL), F32),
        scratch_shapes=[pltpu.VMEM((tm, D_MODEL), BF16),
                        pltpu.VMEM((tm, D_MODEL), F32),
                        pltpu.VMEM((D_MODEL // LANES, tm + SUBLANES * SUBLANES, LANES), F32),
                        pltpu.VMEM((n_edge, D_MODEL), BF16),
                        pltpu.VMEM((n_blocks, n_edge, bn), F32),
                        pltpu.VMEM((n_blocks, n_edge, bn), F32)],
        compiler_params=pltpu.CompilerParams(
            dimension_semantics=("arbitrary", "arbitrary"), vmem_limit_bytes=V7X_VMEM_LIMIT_BYTES),
        name="ffn",
    )(x2d, delta, x_edge, d_edge, p["norm2_g"], p["w_up"], p["w_up"], p["ffn_conv_w"], p["ffn_conv_w"],
      p["w_down"], final_g)


def kernel(x, norm1_g, w_in, w_pool, pool_scale, sgu_norm_g, sgu_w, sgu_b, conv_w, w_pool_out, w_sgu_out,
           w_conv_out, w_o, norm2_g, w_up, ffn_conv_w, w_down, final_g):
    bsz, seq_len, d_model = x.shape
    depth = w_in.shape[0]
    assert d_model == D_MODEL and w_in.shape[2] == OFF_GATE + 3 * D_MODEL and w_up.shape[2] == 2 * D_FF
    tm_in, tm_out, tm_ffn = 512, 1024, 512
    assert all(seq_len % t == 0 and t % SGU_CHUNK == 0 for t in (tm_in, tm_out, tm_ffn))

    p = {
        "norm1_g": norm1_g[:, None, :], "norm2_g": norm2_g[:, None, :],
        "pool_scale": pool_scale[:, None, :], "sgu_norm_g": sgu_norm_g[:, None, :],
        "w_in": w_in.astype(BF16), "w_pool": w_pool.astype(BF16), "sgu_w": sgu_w.astype(BF16),
        "sgu_b": jnp.broadcast_to(jnp.transpose(sgu_b, (0, 2, 1))[..., None],
                                  (depth, N_SGU_GROUPS, SGU_CHUNK, SGU_GROUP)),
        "conv_w": conv_w, "ffn_conv_w": ffn_conv_w,
        "w_pool_out": w_pool_out.astype(BF16), "w_sgu_out": w_sgu_out.astype(BF16),
        "w_conv_out": w_conv_out.astype(BF16), "w_o": w_o.astype(BF16),
        "w_up": w_up.astype(BF16), "w_down": w_down.astype(BF16),
    }
    x2d = x.reshape(bsz * seq_len, d_model)
    for layer in range(depth):
        y, h = _mixer_in(x2d, layer, p, tm=tm_in, seq_len=seq_len)
        delta = _mixer_out(h, y, layer, p, tm=tm_out, bn=256)
        x2d = _ffn(x2d, delta, layer, p, final_g[None, :], tm=tm_ffn, bn=512, seq_len=seq_len,
                   final_norm=(layer == depth - 1))
    return x2d.reshape(bsz, seq_len, d_model)
```

```python
import functools

import jax
import jax.numpy as jnp
from jax import lax
from jax.experimental import pallas as pl
from jax.experimental.pallas import tpu as pltpu

D_MODEL = 2048
D_BRANCH = 1024
POOL_WINDOWS = (2, 4, 8, 16)
POOL_GROUP = D_BRANCH // len(POOL_WINDOWS)
SGU_CHUNK = 128
N_SGU_GROUPS = 8
SGU_GROUP = D_BRANCH // N_SGU_GROUPS
D_FF = 5632
EPS = 1e-6
OFF_GATE = 6 * D_BRANCH

HALO = 16
SUBLANES = 8
LANES = 128
MXU_ACC_ROWS = 512
V7X_VMEM_LIMIT_BYTES = 58 * 1024 * 1024

F32 = jnp.float32
BF16 = jnp.bfloat16


def _rmsnorm(xf, g):
    return xf * lax.rsqrt(jnp.mean(xf * xf, axis=-1, keepdims=True) + EPS) * g


def _dot(a, b):
    return jnp.dot(a, b, preferred_element_type=F32)


def _shift_rows(x, k):
    n = x.shape[0]
    return pltpu.roll(x, (-k) % n, axis=0)


def _norm_with_halo(h_scr, x_main, x_prev, x_next, g, has_prev, has_next, tm):
    h_scr[0:HALO, :] = jnp.where(has_prev, _rmsnorm(x_prev, g), 0.0).astype(BF16)
    h_scr[HALO:HALO + tm, :] = _rmsnorm(x_main, g).astype(BF16)
    h_scr[HALO + tm:, :] = jnp.where(has_next, _rmsnorm(x_next, g), 0.0).astype(BF16)


def _halo_specs(tm, n_rows):
    blocks_per_tile = tm // HALO
    last = n_rows // HALO - 1
    prev = pl.BlockSpec((HALO, D_MODEL), lambda i, s: (jnp.maximum(i * blocks_per_tile - 1, 0), 0))
    nxt = pl.BlockSpec((HALO, D_MODEL), lambda i, s: (jnp.minimum((i + 1) * blocks_per_tile, last), 0))
    return prev, nxt


_MIXER_IN_FIRST_BLOCKS = (0, 1, 5)
_MIXER_IN_SECOND_BLOCKS = (2, 3, 4)


def _mixer_in_kernel(xm_ref, xp_ref, xn_ref, g1_ref, w1_ref, w2_ref, wpool_ref, pscale_ref, sgug_ref,
                     sguw_ref, sgub_ref, convw_ref, y_ref, h_ref, h_scr, cx_scr, vn_scr, z_scr,
                     *, tm, tiles_per_seq, seq_len):
    i = pl.program_id(0)
    s = pl.program_id(1)
    ti = i % tiles_per_seq

    def _pool(win_ref):
        _norm_with_halo(h_scr, xm_ref[...], xp_ref[...], xn_ref[...], g1_ref[...],
                        ti > 0, ti < tiles_per_seq - 1, tm)
        h_ref[...] = h_scr[HALO:HALO + tm, :]
        a = _dot(h_scr[...], win_ref[...])
        pos = ti * tm + lax.broadcasted_iota(jnp.int32, (tm, POOL_GROUP), 0)
        for gi, w in enumerate(POOL_WINDOWS):
            cols = slice(gi * POOL_GROUP, (gi + 1) * POOL_GROUP)
            xg = a[:, cols]
            q, m = xg, 1
            while m < w // 2:
                q = q + _shift_rows(q, m)
                m *= 2
            win_sum = (q + _shift_rows(q, -(w // 2)))[HALO:HALO + tm]
            lo = jnp.maximum(pos - w // 2, 0)
            hi = jnp.minimum(pos + (w - w // 2 - 1), seq_len - 1)
            cnt = (hi - lo + 1).astype(F32)
            pooled = win_sum / cnt - xg[HALO:HALO + tm]
            y = _dot(pooled.astype(BF16), wpool_ref[gi]) * pscale_ref[:, cols]
            y_ref[:, cols] = y.astype(BF16)

    def _sgu_v(win_ref):
        gv = jax.nn.gelu(_dot(h_scr[HALO:HALO + tm, :], win_ref[...]))
        vn_scr[...] = _rmsnorm(gv, sgug_ref[...]).astype(BF16)
        n_chunk = tm // SGU_CHUNK
        for g in range(N_SGU_GROUPS):
            cols = slice(g * SGU_GROUP, (g + 1) * SGU_GROUP)
            rhs = jnp.concatenate(
                [vn_scr[c * SGU_CHUNK:(c + 1) * SGU_CHUNK, cols] for c in range(n_chunk)], axis=1)
            o = _dot(sguw_ref[g], rhs)
            for c in range(n_chunk):
                z_scr[c * SGU_CHUNK:(c + 1) * SGU_CHUNK, cols] = (
                    o[:, c * SGU_GROUP:(c + 1) * SGU_GROUP] + sgub_ref[g])

    def _sgu_u(win_ref):
        u = jax.nn.gelu(_dot(h_scr[HALO:HALO + tm, :], win_ref[...]))
        y_ref[:, D_BRANCH:2 * D_BRANCH] = (u * z_scr[...]).astype(BF16)

    def _conv_x(win_ref):
        cx_scr[...] = _dot(h_scr[...], win_ref[...])

    def _conv_c(win_ref):
        cx_scr[...] = cx_scr[...] * _dot(h_scr[...], win_ref[...])

    def _conv_b(win_ref):
        cx = cx_scr[...]
        conv = (convw_ref[0:1, :] * _shift_rows(cx, -1) + convw_ref[1:2, :] * cx
                + convw_ref[2:3, :] * _shift_rows(cx, 1))[HALO:HALO + tm]
        bg = _dot(h_scr[HALO:HALO + tm, :], win_ref[...])
        y_ref[:, 2 * D_BRANCH:] = (bg * conv).astype(BF16)

    for step, (first, second) in enumerate(((_pool, _sgu_v), (_sgu_u, _conv_x), (_conv_c, _conv_b))):
        @pl.when(s == step)
        def _step(first=first, second=second):
            first(w1_ref)
            second(w2_ref)


def _mixer_in(x2d, layer, p, *, tm, seq_len):
    n_rows = x2d.shape[0]
    n_steps = len(_MIXER_IN_FIRST_BLOCKS)

    def win_spec(blocks):
        def index(i, s):
            blk = jnp.int32(blocks[-1])
            for step in range(n_steps - 2, -1, -1):
                blk = jnp.where(s == step, blocks[step], blk)
            return (layer, 0, blk)
        return pl.BlockSpec((None, D_MODEL, D_BRANCH), index)

    prev_spec, next_spec = _halo_specs(tm, n_rows)
    const = lambda *shape: pl.BlockSpec((None,) + shape, lambda i, s: (layer,) + (0,) * len(shape))
    kernel = functools.partial(_mixer_in_kernel, tm=tm, tiles_per_seq=seq_len // tm, seq_len=seq_len)
    return pl.pallas_call(
        kernel,
        grid=(n_rows // tm, n_steps),
        in_specs=[
            pl.BlockSpec((tm, D_MODEL), lambda i, s: (i, 0)),
            prev_spec, next_spec,
            const(1, D_MODEL),
            win_spec(_MIXER_IN_FIRST_BLOCKS), win_spec(_MIXER_IN_SECOND_BLOCKS),
            const(len(POOL_WINDOWS), POOL_GROUP, POOL_GROUP),
            const(1, D_BRANCH),
            const(1, D_BRANCH),
            const(N_SGU_GROUPS, SGU_CHUNK, SGU_CHUNK),
            const(N_SGU_GROUPS, SGU_CHUNK, SGU_GROUP),
            const(3, D_BRANCH),
        ],
        out_specs=[pl.BlockSpec((tm, 3 * D_BRANCH), lambda i, s: (i, 0)),
                   pl.BlockSpec((tm, D_MODEL), lambda i, s: (i, 0))],
        out_shape=[jax.ShapeDtypeStruct((n_rows, 3 * D_BRANCH), BF16),
                   jax.ShapeDtypeStruct((n_rows, D_MODEL), BF16)],
        scratch_shapes=[
            pltpu.VMEM((tm + 2 * HALO, D_MODEL), BF16),
            pltpu.VMEM((tm + 2 * HALO, D_BRANCH), F32),
            pltpu.VMEM((tm, D_BRANCH), BF16),
            pltpu.VMEM((tm, D_BRANCH), F32),
        ],
        compiler_params=pltpu.CompilerParams(
            dimension_semantics=("parallel", "arbitrary"), vmem_limit_bytes=V7X_VMEM_LIMIT_BYTES),
        name="mixer_in",
    )(x2d, x2d, x2d, p["norm1_g"], p["w_in"], p["w_in"], p["w_pool"], p["pool_scale"], p["sgu_norm_g"],
      p["sgu_w"], p["sgu_b"], p["conv_w"])


def _mixer_out_kernel(h_ref, y_ref, wga_ref, wgb_ref, wgc_ref, wa_ref, wb_ref, wc_ref, wo_ref, out_ref,
                      *, bn_acc):
    s = pl.program_id(1)

    def step(first):
        for r0 in range(0, h_ref.shape[0], MXU_ACC_ROWS):
            rows = slice(r0, r0 + MXU_ACC_ROWS)
            h = h_ref[rows, :]
            merged = (jax.nn.sigmoid(_dot(h, wga_ref[...])) * _dot(y_ref[rows, 0:D_BRANCH], wa_ref[...])
                      + jax.nn.sigmoid(_dot(h, wgb_ref[...]))
                      * _dot(y_ref[rows, D_BRANCH:2 * D_BRANCH], wb_ref[...])
                      + jax.nn.sigmoid(_dot(h, wgc_ref[...])) * _dot(y_ref[rows, 2 * D_BRANCH:], wc_ref[...]))
            merged = merged.astype(BF16)
            for c in range(D_MODEL // bn_acc):
                cols = slice(c * bn_acc, (c + 1) * bn_acc)
                upd = _dot(merged, wo_ref[:, cols])
                if first:
                    out_ref[rows, cols] = upd
                else:
                    out_ref[rows, cols] += upd

    pl.when(s == 0)(functools.partial(step, True))
    pl.when(s > 0)(functools.partial(step, False))


def _mixer_out(h, y, layer, p, *, tm, bn):
    n_rows = h.shape[0]
    n_steps = D_MODEL // bn

    def gate_spec(branch):
        first = (OFF_GATE + branch * D_MODEL) // bn
        return pl.BlockSpec((None, D_MODEL, bn), lambda i, s: (layer, 0, first + s))

    branch_spec = pl.BlockSpec((None, D_BRANCH, bn), lambda i, s: (layer, 0, s))
    kernel = functools.partial(_mixer_out_kernel, bn_acc=512)
    return pl.pallas_call(
        kernel,
        grid=(n_rows // tm, n_steps),
        in_specs=[
            pl.BlockSpec((tm, D_MODEL), lambda i, s: (i, 0)),
            pl.BlockSpec((tm, 3 * D_BRANCH), lambda i, s: (i, 0)),
            gate_spec(0), gate_spec(1), gate_spec(2),
            branch_spec, branch_spec, branch_spec,
            pl.BlockSpec((None, bn, D_MODEL), lambda i, s: (layer, s, 0)),
        ],
        out_specs=pl.BlockSpec((tm, D_MODEL), lambda i, s: (i, 0)),
        out_shape=jax.ShapeDtypeStruct((n_rows, D_MODEL), F32),
        compiler_params=pltpu.CompilerParams(
            dimension_semantics=("parallel", "arbitrary"), vmem_limit_bytes=V7X_VMEM_LIMIT_BYTES),
        name="mixer_out",
    )(h, y, p["w_in"], p["w_in"], p["w_in"], p["w_pool_out"], p["w_sgu_out"], p["w_conv_out"], p["w_o"])


def _ffn_kernel(xm_ref, dm_ref, xe_ref, de_ref, g2_ref, wg_ref, wv_ref, cwg_ref, cwv_ref, wd_ref, gf_ref,
                out_ref, h_scr, acc_scr, cols_scr, he_scr, eg_scr, ev_scr,
                *, tm, n_tiles, tiles_per_seq, n_blocks, final_norm):
    i = pl.program_id(0)
    s = pl.program_id(1)
    ti = i % tiles_per_seq
    n_slabs = tm // SUBLANES

    n_lane_tiles = D_MODEL // LANES
    run_pitch = n_slabs + SUBLANES

    def strided_rows(start, stride):
        return jnp.concatenate(
            [cols_scr[j, pl.ds(start, SUBLANES, stride=stride), :] for j in range(n_lane_tiles)], axis=1)

    @pl.when(s == 0)
    def _init():
        for j in range(n_lane_tiles):
            cols = slice(j * LANES, (j + 1) * LANES)
            for a in range(SUBLANES):
                src = slice(a * n_slabs, (a + 1) * n_slabs)
                cols_scr[j, a * run_pitch:a * run_pitch + n_slabs, :] = xm_ref[src, cols] + dm_ref[src, cols]

        def slab(b, carry):
            acc_scr[pl.ds(pl.multiple_of(SUBLANES * b, SUBLANES), SUBLANES), :] = strided_rows(b, run_pitch)
            return carry
        lax.fori_loop(0, n_slabs, slab, 0, unroll=8)
        h_scr[...] = _rmsnorm(acc_scr[...], g2_ref[...]).astype(BF16)

    @pl.when((s == 0) & (i == 0))
    def _edge_norm():
        he_scr[...] = _rmsnorm(xe_ref[...] + de_ref[...], g2_ref[...]).astype(BF16)

    @pl.when(i == 0)
    def _edge_up():
        eg_scr[s] = _dot(he_scr[...], wg_ref[...])
        ev_scr[s] = _dot(he_scr[...], wv_ref[...])

    sub = lax.broadcasted_iota(jnp.int32, (SUBLANES, 1), 0)

    def conv3(up, edge_scr, cw_ref):
        before = jnp.where(ti > 0, edge_scr[s, pl.ds(i, 1), :], 0.0)
        after = jnp.where(ti < tiles_per_seq - 1, edge_scr[s, pl.ds(n_tiles + i, 1), :], 0.0)
        prev0 = jnp.where(sub == 0, before, pltpu.roll(up[tm - SUBLANES:, :], 1, axis=0))
        next_last = jnp.where(sub == SUBLANES - 1, after, pltpu.roll(up[:SUBLANES, :], SUBLANES - 1, axis=0))
        prev = jnp.concatenate([prev0, up[:tm - SUBLANES, :]], axis=0)
        nxt = jnp.concatenate([up[SUBLANES:, :], next_last], axis=0)
        return cw_ref[0:1, :] * prev + cw_ref[1:2, :] * up + cw_ref[2:3, :] * nxt

    h = h_scr[...]
    gate = conv3(_dot(h, wg_ref[...]), eg_scr, cwg_ref)
    val = conv3(_dot(h, wv_ref[...]), ev_scr, cwv_ref)
    act = (jax.nn.silu(gate) * val).astype(BF16)
    acc_scr[...] += _dot(act, wd_ref[...])

    @pl.when(s == n_blocks - 1)
    def _store():
        for j in range(n_lane_tiles):
            cols_scr[j, 0:tm, :] = acc_scr[:, j * LANES:(j + 1) * LANES]

        def slab(m, carry):
            per = n_slabs // SUBLANES
            rows = strided_rows(SUBLANES * SUBLANES * (m % per) + m // per, SUBLANES)
            if final_norm:
                rows = _rmsnorm(rows, gf_ref[...])
            out_ref[pl.ds(pl.multiple_of(SUBLANES * m, SUBLANES), SUBLANES), :] = rows
            return carry
        lax.fori_loop(0, n_slabs, slab, 0, unroll=8)


def _ffn(x2d, delta, layer, p, final_g, *, tm, bn, seq_len, final_norm):
    n_rows = x2d.shape[0]
    n_tiles = n_rows // tm
    n_blocks = D_FF // bn
    starts = [t * tm for t in range(n_tiles)]
    edge_rows = jnp.asarray([max(r - 1, 0) for r in starts] + [min(r + tm, n_rows - 1) for r in starts],
                            dtype=jnp.int32)
    x_edge = jnp.take(x2d, edge_rows, axis=0)
    d_edge = jnp.take(delta, edge_rows, axis=0)
    n_edge = 2 * n_tiles

    main_spec = pl.BlockSpec((tm, D_MODEL), lambda i, s: (i, 0))
    edge_spec = pl.BlockSpec((n_edge, D_MODEL), lambda i, s: (0, 0))
    kernel = functools.partial(_ffn_kernel, tm=tm, n_tiles=n_tiles, tiles_per_seq=seq_len // tm,
                               n_blocks=n_blocks, final_norm=final_norm)
    return pl.pallas_call(
        kernel,
        grid=(n_tiles, n_blocks),
        in_specs=[
            main_spec, main_spec,
            edge_spec, edge_spec,
            pl.BlockSpec((None, 1, D_MODEL), lambda i, s: (layer, 0, 0)),
            pl.BlockSpec((None, D_MODEL, bn), lambda i, s: (layer, 0, s)),
            pl.BlockSpec((None, D_MODEL, bn), lambda i, s: (layer, 0, n_blocks + s)),
            pl.BlockSpec((None, 3, bn), lambda i, s: (layer, 0, s)),
            pl.BlockSpec((None, 3, bn), lambda i, s: (layer, 0, n_blocks + s)),
            pl.BlockSpec((None, bn, D_MODEL), lambda i, s: (layer, s, 0)),
            pl.BlockSpec((1, D_MODEL), lambda i, s: (0, 0)),
        ],
        out_specs=main_spec,
        out_shape=jax.ShapeDtypeStruct((n_rows, D_MODEL), F32),
        scratch_shapes=[pltpu.VMEM((tm, D_MODEL), BF16),
                        pltpu.VMEM((tm, D_MODEL), F32),
                        pltpu.VMEM((D_MODEL // LANES, tm + SUBLANES * SUBLANES, LANES), F32),
                        pltpu.VMEM((n_edge, D_MODEL), BF16),
                        pltpu.VMEM((n_blocks, n_edge, bn), F32),
                        pltpu.VMEM((n_blocks, n_edge, bn), F32)],
        compiler_params=pltpu.CompilerParams(
            dimension_semantics=("arbitrary", "arbitrary"), vmem_limit_bytes=V7X_VMEM_LIMIT_BYTES),
        name="ffn",
    )(x2d, delta, x_edge, d_edge, p["norm2_g"], p["w_up"], p["w_up"], p["ffn_conv_w"], p["ffn_conv_w"],
      p["w_down"], final_g)


def kernel(x, norm1_g, w_in, w_pool, pool_scale, sgu_norm_g, sgu_w, sgu_b, conv_w, w_pool_out, w_sgu_out,
           w_conv_out, w_o, norm2_g, w_up, ffn_conv_w, w_down, final_g):
    bsz, seq_len, d_model = x.shape
    depth = w_in.shape[0]
    assert d_model == D_MODEL and w_in.shape[2] == OFF_GATE + 3 * D_MODEL and w_up.shape[2] == 2 * D_FF
    tm_in, tm_out, tm_ffn = 512, 1024, 512
    assert all(seq_len % t == 0 and t % SGU_CHUNK == 0 for t in (tm_in, tm_out, tm_ffn))

    p = {
        "norm1_g": norm1_g[:, None, :], "norm2_g": norm2_g[:, None, :],
        "pool_scale": pool_scale[:, None, :], "sgu_norm_g": sgu_norm_g[:, None, :],
        "w_in": w_in.astype(BF16), "w_pool": w_pool.astype(BF16), "sgu_w": sgu_w.astype(BF16),
        "sgu_b": jnp.broadcast_to(jnp.transpose(sgu_b, (0, 2, 1))[..., None],
                                  (depth, N_SGU_GROUPS, SGU_CHUNK, SGU_GROUP)),
        "conv_w": conv_w, "ffn_conv_w": ffn_conv_w,
        "w_pool_out": w_pool_out.astype(BF16), "w_sgu_out": w_sgu_out.astype(BF16),
        "w_conv_out": w_conv_out.astype(BF16), "w_o": w_o.astype(BF16),
        "w_up": w_up.astype(BF16), "w_down": w_down.astype(BF16),
    }
    x2d = x.reshape(bsz * seq_len, d_model)
    for layer in range(depth):
        y, h = _mixer_in(x2d, layer, p, tm=tm_in, seq_len=seq_len)
        delta = _mixer_out(h, y, layer, p, tm=tm_out, bn=256)
        x2d = _ffn(x2d, delta, layer, p, final_g[None, :], tm=tm_ffn, bn=512, seq_len=seq_len,
                   final_norm=(layer == depth - 1))
    return x2d.reshape(bsz, seq_len, d_model)
```

```python
import functools

import jax
import jax.numpy as jnp
from jax import lax
from jax.experimental import pallas as pl
from jax.experimental.pallas import tpu as pltpu

D_MODEL = 2048
D_BRANCH = 1024
POOL_WINDOWS = (2, 4, 8, 16)
POOL_GROUP = D_BRANCH // len(POOL_WINDOWS)
SGU_CHUNK = 128
N_SGU_GROUPS = 8
SGU_GROUP = D_BRANCH // N_SGU_GROUPS
D_FF = 5632
EPS = 1e-6
OFF_GATE = 6 * D_BRANCH

HALO = 16
SUBLANES = 8
LANES = 128
MXU_ACC_ROWS = 512
V7X_VMEM_LIMIT_BYTES = 58 * 1024 * 1024

F32 = jnp.float32
BF16 = jnp.bfloat16


def _rmsnorm(xf, g):
    return xf * lax.rsqrt(jnp.mean(xf * xf, axis=-1, keepdims=True) + EPS) * g


def _dot(a, b):
    return jnp.dot(a, b, preferred_element_type=F32)


def _shift_rows(x, k):
    n = x.shape[0]
    return pltpu.roll(x, (-k) % n, axis=0)


def _norm_with_halo(h_scr, x_main, x_prev, x_next, g, has_prev, has_next, tm):
    h_scr[0:HALO, :] = jnp.where(has_prev, _rmsnorm(x_prev, g), 0.0).astype(BF16)
    h_scr[HALO:HALO + tm, :] = _rmsnorm(x_main, g).astype(BF16)
    h_scr[HALO + tm:, :] = jnp.where(has_next, _rmsnorm(x_next, g), 0.0).astype(BF16)


def _halo_specs(tm, n_rows):
    blocks_per_tile = tm // HALO
    last = n_rows // HALO - 1
    prev = pl.BlockSpec((HALO, D_MODEL), lambda i, s: (jnp.maximum(i * blocks_per_tile - 1, 0), 0))
    nxt = pl.BlockSpec((HALO, D_MODEL), lambda i, s: (jnp.minimum((i + 1) * blocks_per_tile, last), 0))
    return prev, nxt


_MIXER_IN_STEP_BLOCKS = ((0, 2, 1), (3, 5, 4))


def _mixer_in_kernel(xm_ref, xp_ref, xn_ref, g1_ref, wa_ref, wb_ref, wc_ref, wpool_ref, pscale_ref, sgug_ref,
                     sguw_ref, sgub_ref, convw_ref, y_ref, h_ref, h_scr, cx_scr, vn_scr, z_scr,
                     *, tm, tiles_per_seq, seq_len):
    i = pl.program_id(0)
    s = pl.program_id(1)
    ti = i % tiles_per_seq

    def _pool(win_ref):
        _norm_with_halo(h_scr, xm_ref[...], xp_ref[...], xn_ref[...], g1_ref[...],
                        ti > 0, ti < tiles_per_seq - 1, tm)
        h_ref[...] = h_scr[HALO:HALO + tm, :]
        a = _dot(h_scr[...], win_ref[...])
        pos = ti * tm + lax.broadcasted_iota(jnp.int32, (tm, POOL_GROUP), 0)
        for gi, w in enumerate(POOL_WINDOWS):
            cols = slice(gi * POOL_GROUP, (gi + 1) * POOL_GROUP)
            xg = a[:, cols]
            q, m = xg, 1
            while m < w // 2:
                q = q + _shift_rows(q, m)
                m *= 2
            win_sum = (q + _shift_rows(q, -(w // 2)))[HALO:HALO + tm]
            lo = jnp.maximum(pos - w // 2, 0)
            hi = jnp.minimum(pos + (w - w // 2 - 1), seq_len - 1)
            cnt = (hi - lo + 1).astype(F32)
            pooled = win_sum / cnt - xg[HALO:HALO + tm]
            y = _dot(pooled.astype(BF16), wpool_ref[gi]) * pscale_ref[:, cols]
            y_ref[:, cols] = y.astype(BF16)

    def _sgu_v(win_ref):
        gv = jax.nn.gelu(_dot(h_scr[HALO:HALO + tm, :], win_ref[...]))
        vn_scr[...] = _rmsnorm(gv, sgug_ref[...]).astype(BF16)
        n_chunk = tm // SGU_CHUNK
        for g in range(N_SGU_GROUPS):
            cols = slice(g * SGU_GROUP, (g + 1) * SGU_GROUP)
            rhs = jnp.concatenate(
                [vn_scr[c * SGU_CHUNK:(c + 1) * SGU_CHUNK, cols] for c in range(n_chunk)], axis=1)
            o = _dot(sguw_ref[g], rhs)
            for c in range(n_chunk):
                z_scr[c * SGU_CHUNK:(c + 1) * SGU_CHUNK, cols] = (
                    o[:, c * SGU_GROUP:(c + 1) * SGU_GROUP] + sgub_ref[g])

    def _sgu_u(win_ref):
        u = jax.nn.gelu(_dot(h_scr[HALO:HALO + tm, :], win_ref[...]))
        y_ref[:, D_BRANCH:2 * D_BRANCH] = (u * z_scr[...]).astype(BF16)

    def _conv_x(win_ref):
        cx_scr[...] = _dot(h_scr[...], win_ref[...])

    def _conv_c(win_ref):
        cx_scr[...] = cx_scr[...] * _dot(h_scr[...], win_ref[...])

    def _conv_b(win_ref):
        cx = cx_scr[...]
        conv = (convw_ref[0:1, :] * _shift_rows(cx, -1) + convw_ref[1:2, :] * cx
                + convw_ref[2:3, :] * _shift_rows(cx, 1))[HALO:HALO + tm]
        bg = _dot(h_scr[HALO:HALO + tm, :], win_ref[...])
        y_ref[:, 2 * D_BRANCH:] = (bg * conv).astype(BF16)

    for step, sections in enumerate(((_pool, _sgu_v, _sgu_u), (_conv_x, _conv_c, _conv_b))):
        @pl.when(s == step)
        def _step(sections=sections):
            for section, w_ref in zip(sections, (wa_ref, wb_ref, wc_ref)):
                section(w_ref)


def _mixer_in(x2d, layer, p, *, tm, seq_len):
    n_rows = x2d.shape[0]
    n_steps = len(_MIXER_IN_STEP_BLOCKS)

    def win_spec(k):
        blocks = [step_blocks[k] for step_blocks in _MIXER_IN_STEP_BLOCKS]

        def index(i, s):
            blk = jnp.int32(blocks[-1])
            for step in range(n_steps - 2, -1, -1):
                blk = jnp.where(s == step, blocks[step], blk)
            return (layer, 0, blk)
        return pl.BlockSpec((None, D_MODEL, D_BRANCH), index)

    prev_spec, next_spec = _halo_specs(tm, n_rows)
    const = lambda *shape: pl.BlockSpec((None,) + shape, lambda i, s: (layer,) + (0,) * len(shape))
    kernel = functools.partial(_mixer_in_kernel, tm=tm, tiles_per_seq=seq_len // tm, seq_len=seq_len)
    return pl.pallas_call(
        kernel,
        grid=(n_rows // tm, n_steps),
        in_specs=[
            pl.BlockSpec((tm, D_MODEL), lambda i, s: (i, 0)),
            prev_spec, next_spec,
            const(1, D_MODEL),
            win_spec(0), win_spec(1), win_spec(2),
            const(len(POOL_WINDOWS), POOL_GROUP, POOL_GROUP),
            const(1, D_BRANCH),
            const(1, D_BRANCH),
            const(N_SGU_GROUPS, SGU_CHUNK, SGU_CHUNK),
            const(N_SGU_GROUPS, SGU_CHUNK, SGU_GROUP),
            const(3, D_BRANCH),
        ],
        out_specs=[pl.BlockSpec((tm, 3 * D_BRANCH), lambda i, s: (i, 0)),
                   pl.BlockSpec((tm, D_MODEL), lambda i, s: (i, 0))],
        out_shape=[jax.ShapeDtypeStruct((n_rows, 3 * D_BRANCH), BF16),
                   jax.ShapeDtypeStruct((n_rows, D_MODEL), BF16)],
        scratch_shapes=[
            pltpu.VMEM((tm + 2 * HALO, D_MODEL), BF16),
            pltpu.VMEM((tm + 2 * HALO, D_BRANCH), F32),
            pltpu.VMEM((tm, D_BRANCH), BF16),
            pltpu.VMEM((tm, D_BRANCH), F32),
        ],
        compiler_params=pltpu.CompilerParams(
            dimension_semantics=("parallel", "arbitrary"), vmem_limit_bytes=V7X_VMEM_LIMIT_BYTES),
        name="mixer_in",
    )(x2d, x2d, x2d, p["norm1_g"], p["w_in"], p["w_in"], p["w_in"], p["w_pool"], p["pool_scale"],
      p["sgu_norm_g"],
      p["sgu_w"], p["sgu_b"], p["conv_w"])


def _mixer_out_kernel(h_ref, y_ref, wga_ref, wgb_ref, wgc_ref, wa_ref, wb_ref, wc_ref, wo_ref, out_ref,
                      *, bn_acc):
    s = pl.program_id(1)

    def step(first):
        for r0 in range(0, h_ref.shape[0], MXU_ACC_ROWS):
            rows = slice(r0, r0 + MXU_ACC_ROWS)
            h = h_ref[rows, :]
            merged = (jax.nn.sigmoid(_dot(h, wga_ref[...])) * _dot(y_ref[rows, 0:D_BRANCH], wa_ref[...])
                      + jax.nn.sigmoid(_dot(h, wgb_ref[...]))
                      * _dot(y_ref[rows, D_BRANCH:2 * D_BRANCH], wb_ref[...])
                      + jax.nn.sigmoid(_dot(h, wgc_ref[...])) * _dot(y_ref[rows, 2 * D_BRANCH:], wc_ref[...]))
            merged = merged.astype(BF16)
            for c in range(D_MODEL // bn_acc):
                cols = slice(c * bn_acc, (c + 1) * bn_acc)
                upd = _dot(merged, wo_ref[:, cols])
                if first:
                    out_ref[rows, cols] = upd
                else:
                    out_ref[rows, cols] += upd

    pl.when(s == 0)(functools.partial(step, True))
    pl.when(s > 0)(functools.partial(step, False))


def _mixer_out(h, y, layer, p, *, tm, bn):
    n_rows = h.shape[0]
    n_steps = D_MODEL // bn

    def gate_spec(branch):
        first = (OFF_GATE + branch * D_MODEL) // bn
        return pl.BlockSpec((None, D_MODEL, bn), lambda i, s: (layer, 0, first + s))

    branch_spec = pl.BlockSpec((None, D_BRANCH, bn), lambda i, s: (layer, 0, s))
    kernel = functools.partial(_mixer_out_kernel, bn_acc=512)
    return pl.pallas_call(
        kernel,
        grid=(n_rows // tm, n_steps),
        in_specs=[
            pl.BlockSpec((tm, D_MODEL), lambda i, s: (i, 0)),
            pl.BlockSpec((tm, 3 * D_BRANCH), lambda i, s: (i, 0)),
            gate_spec(0), gate_spec(1), gate_spec(2),
            branch_spec, branch_spec, branch_spec,
            pl.BlockSpec((None, bn, D_MODEL), lambda i, s: (layer, s, 0)),
        ],
        out_specs=pl.BlockSpec((tm, D_MODEL), lambda i, s: (i, 0)),
        out_shape=jax.ShapeDtypeStruct((n_rows, D_MODEL), F32),
        compiler_params=pltpu.CompilerParams(
            dimension_semantics=("parallel", "arbitrary"), vmem_limit_bytes=V7X_VMEM_LIMIT_BYTES),
        name="mixer_out",
    )(h, y, p["w_in"], p["w_in"], p["w_in"], p["w_pool_out"], p["w_sgu_out"], p["w_conv_out"], p["w_o"])


def _ffn_kernel(xm_ref, dm_ref, xe_ref, de_ref, g2_ref, wg_ref, wv_ref, cw_ref, wd_ref, gf_ref,
                out_ref, h_scr, acc_scr, cols_scr, he_scr, eg_scr, ev_scr,
                *, tm, n_tiles, tiles_per_seq, n_blocks, final_norm):
    i = pl.program_id(0)
    s = pl.program_id(1)
    ti = i % tiles_per_seq
    n_slabs = tm // SUBLANES

    n_lane_tiles = D_MODEL // LANES
    run_pitch = n_slabs + SUBLANES

    def strided_rows(start, stride):
        return jnp.concatenate(
            [cols_scr[j, pl.ds(start, SUBLANES, stride=stride), :] for j in range(n_lane_tiles)], axis=1)

    @pl.when(s == 0)
    def _init():
        for j in range(n_lane_tiles):
            cols = slice(j * LANES, (j + 1) * LANES)
            for a in range(SUBLANES):
                src = slice(a * n_slabs, (a + 1) * n_slabs)
                cols_scr[j, a * run_pitch:a * run_pitch + n_slabs, :] = xm_ref[src, cols] + dm_ref[src, cols]

        def slab(b, carry):
            acc_scr[pl.ds(pl.multiple_of(SUBLANES * b, SUBLANES), SUBLANES), :] = strided_rows(b, run_pitch)
            return carry
        lax.fori_loop(0, n_slabs, slab, 0, unroll=8)
        h_scr[...] = _rmsnorm(acc_scr[...], g2_ref[...]).astype(BF16)

    @pl.when((s == 0) & (i == 0))
    def _edge_norm():
        he_scr[...] = _rmsnorm(xe_ref[...] + de_ref[...], g2_ref[...]).astype(BF16)

    @pl.when(i == 0)
    def _edge_up():
        eg_scr[s] = _dot(he_scr[...], wg_ref[...])
        ev_scr[s] = _dot(he_scr[...], wv_ref[...])

    sub = lax.broadcasted_iota(jnp.int32, (SUBLANES, 1), 0)

    def conv3(up, edge_scr, blk):
        cw = cw_ref.at[blk]
        before = jnp.where(ti > 0, edge_scr[s, pl.ds(i, 1), :], 0.0)
        after = jnp.where(ti < tiles_per_seq - 1, edge_scr[s, pl.ds(n_tiles + i, 1), :], 0.0)
        prev0 = jnp.where(sub == 0, before, pltpu.roll(up[tm - SUBLANES:, :], 1, axis=0))
        next_last = jnp.where(sub == SUBLANES - 1, after, pltpu.roll(up[:SUBLANES, :], SUBLANES - 1, axis=0))
        prev = jnp.concatenate([prev0, up[:tm - SUBLANES, :]], axis=0)
        nxt = jnp.concatenate([up[SUBLANES:, :], next_last], axis=0)
        return cw[0:1, :] * prev + cw[1:2, :] * up + cw[2:3, :] * nxt

    h = h_scr[...]
    gate = conv3(_dot(h, wg_ref[...]), eg_scr, s)
    val = conv3(_dot(h, wv_ref[...]), ev_scr, n_blocks + s)
    act = (jax.nn.silu(gate) * val).astype(BF16)
    acc_scr[...] += _dot(act, wd_ref[...])

    @pl.when(s == n_blocks - 1)
    def _store():
        for j in range(n_lane_tiles):
            cols_scr[j, 0:tm, :] = acc_scr[:, j * LANES:(j + 1) * LANES]

        def slab(m, carry):
            per = n_slabs // SUBLANES
            rows = strided_rows(SUBLANES * SUBLANES * (m % per) + m // per, SUBLANES)
            if final_norm:
                rows = _rmsnorm(rows, gf_ref[...])
            out_ref[pl.ds(pl.multiple_of(SUBLANES * m, SUBLANES), SUBLANES), :] = rows
            return carry
        lax.fori_loop(0, n_slabs, slab, 0, unroll=8)


def _ffn(x2d, delta, layer, p, final_g, *, tm, bn, seq_len, final_norm):
    n_rows = x2d.shape[0]
    n_tiles = n_rows // tm
    n_blocks = D_FF // bn
    starts = [t * tm for t in range(n_tiles)]
    edge_rows = jnp.asarray([max(r - 1, 0) for r in starts] + [min(r + tm, n_rows - 1) for r in starts],
                            dtype=jnp.int32)
    x_edge = jnp.take(x2d, edge_rows, axis=0)
    d_edge = jnp.take(delta, edge_rows, axis=0)
    n_edge = 2 * n_tiles
    cw = p["ffn_conv_w"]
    conv_w_blocks = jnp.transpose(cw.reshape(cw.shape[0], 3, 2 * n_blocks, bn), (0, 2, 1, 3))

    main_spec = pl.BlockSpec((tm, D_MODEL), lambda i, s: (i, 0))
    edge_spec = pl.BlockSpec((n_edge, D_MODEL), lambda i, s: (0, 0))
    kernel = functools.partial(_ffn_kernel, tm=tm, n_tiles=n_tiles, tiles_per_seq=seq_len // tm,
                               n_blocks=n_blocks, final_norm=final_norm)
    return pl.pallas_call(
        kernel,
        grid=(n_tiles, n_blocks),
        in_specs=[
            main_spec, main_spec,
            edge_spec, edge_spec,
            pl.BlockSpec((None, 1, D_MODEL), lambda i, s: (layer, 0, 0)),
            pl.BlockSpec((None, D_MODEL, bn), lambda i, s: (layer, 0, s)),
            pl.BlockSpec((None, D_MODEL, bn), lambda i, s: (layer, 0, n_blocks + s)),
            pl.BlockSpec((None, 2 * n_blocks, 3, bn), lambda i, s: (layer, 0, 0, 0)),
            pl.BlockSpec((None, bn, D_MODEL), lambda i, s: (layer, s, 0)),
            pl.BlockSpec((1, D_MODEL), lambda i, s: (0, 0)),
        ],
        out_specs=main_spec,
        out_shape=jax.ShapeDtypeStruct((n_rows, D_MODEL), F32),
        scratch_shapes=[pltpu.VMEM((tm, D_MODEL), BF16),
                        pltpu.VMEM((tm, D_MODEL), F32),
                        pltpu.VMEM((D_MODEL // LANES, tm + SUBLANES * SUBLANES, LANES), F32),
                        pltpu.VMEM((n_edge, D_MODEL), BF16),
                        pltpu.VMEM((n_blocks, n_edge, bn), F32),
                        pltpu.VMEM((n_blocks, n_edge, bn), F32)],
        compiler_params=pltpu.CompilerParams(
            dimension_semantics=("arbitrary", "arbitrary"), vmem_limit_bytes=V7X_VMEM_LIMIT_BYTES),
        name="ffn",
    )(x2d, delta, x_edge, d_edge, p["norm2_g"], p["w_up"], p["w_up"], conv_w_blocks, p["w_down"], final_g)


def kernel(x, norm1_g, w_in, w_pool, pool_scale, sgu_norm_g, sgu_w, sgu_b, conv_w, w_pool_out, w_sgu_out,
           w_conv_out, w_o, norm2_g, w_up, ffn_conv_w, w_down, final_g):
    bsz, seq_len, d_model = x.shape
    depth = w_in.shape[0]
    assert d_model == D_MODEL and w_in.shape[2] == OFF_GATE + 3 * D_MODEL and w_up.shape[2] == 2 * D_FF
    tm_in, tm_out, tm_ffn = 512, 1024, 512
    assert all(seq_len % t == 0 and t % SGU_CHUNK == 0 for t in (tm_in, tm_out, tm_ffn))

    p = {
        "norm1_g": norm1_g[:, None, :], "norm2_g": norm2_g[:, None, :],
        "pool_scale": pool_scale[:, None, :], "sgu_norm_g": sgu_norm_g[:, None, :],
        "w_in": w_in.astype(BF16), "w_pool": w_pool.astype(BF16), "sgu_w": sgu_w.astype(BF16),
        "sgu_b": jnp.broadcast_to(jnp.transpose(sgu_b, (0, 2, 1))[..., None],
                                  (depth, N_SGU_GROUPS, SGU_CHUNK, SGU_GROUP)),
        "conv_w": conv_w, "ffn_conv_w": ffn_conv_w,
        "w_pool_out": w_pool_out.astype(BF16), "w_sgu_out": w_sgu_out.astype(BF16),
        "w_conv_out": w_conv_out.astype(BF16), "w_o": w_o.astype(BF16),
        "w_up": w_up.astype(BF16), "w_down": w_down.astype(BF16),
    }
    x2d = x.reshape(bsz * seq_len, d_model)
    for layer in range(depth):
        y, h = _mixer_in(x2d, layer, p, tm=tm_in, seq_len=seq_len)
        delta = _mixer_out(h, y, layer, p, tm=tm_out, bn=256)
        x2d = _ffn(x2d, delta, layer, p, final_g[None, :], tm=tm_ffn, bn=512, seq_len=seq_len,
                   final_norm=(layer == depth - 1))
    return x2d.reshape(bsz, seq_len, d_model)
```

```python
import functools

import jax
import jax.numpy as jnp
from jax import lax
from jax.experimental import pallas as pl
from jax.experimental.pallas import tpu as pltpu

D_MODEL = 2048
D_BRANCH = 1024
POOL_WINDOWS = (2, 4, 8, 16)
POOL_GROUP = D_BRANCH // len(POOL_WINDOWS)
SGU_CHUNK = 128
N_SGU_GROUPS = 8
SGU_GROUP = D_BRANCH // N_SGU_GROUPS
D_FF = 5632
EPS = 1e-6
OFF_GATE = 6 * D_BRANCH

HALO = 16
SUBLANES = 8
LANES = 128
MXU_ACC_ROWS = 512
V7X_VMEM_LIMIT_BYTES = 58 * 1024 * 1024

F32 = jnp.float32
BF16 = jnp.bfloat16


def _rmsnorm(xf, g):
    return xf * lax.rsqrt(jnp.mean(xf * xf, axis=-1, keepdims=True) + EPS) * g


def _dot(a, b):
    return jnp.dot(a, b, preferred_element_type=F32)


def _shift_rows(x, k):
    n = x.shape[0]
    return pltpu.roll(x, (-k) % n, axis=0)


def _norm_with_halo(h_scr, x_main, x_prev, x_next, g, has_prev, has_next, tm):
    h_scr[0:HALO, :] = jnp.where(has_prev, _rmsnorm(x_prev, g), 0.0).astype(BF16)
    h_scr[HALO:HALO + tm, :] = _rmsnorm(x_main, g).astype(BF16)
    h_scr[HALO + tm:, :] = jnp.where(has_next, _rmsnorm(x_next, g), 0.0).astype(BF16)


def _halo_specs(tm, n_rows):
    blocks_per_tile = tm // HALO
    last = n_rows // HALO - 1
    prev = pl.BlockSpec((HALO, D_MODEL), lambda i, s: (jnp.maximum(i * blocks_per_tile - 1, 0), 0))
    nxt = pl.BlockSpec((HALO, D_MODEL), lambda i, s: (jnp.minimum((i + 1) * blocks_per_tile, last), 0))
    return prev, nxt


_MIXER_IN_STEP_BLOCKS = ((0, 2, 1), (3, 5, 4))


def _mixer_in_kernel(xm_ref, xp_ref, xn_ref, g1_ref, wa_ref, wb_ref, wc_ref, wpool_ref, pscale_ref, sgug_ref,
                     sguw_ref, sgub_ref, convw_ref, y_ref, h_ref, h_scr, cx_scr, vn_scr, z_scr,
                     *, tm, tiles_per_seq, seq_len):
    i = pl.program_id(0)
    s = pl.program_id(1)
    ti = i % tiles_per_seq

    def _pool(win_ref):
        _norm_with_halo(h_scr, xm_ref[...], xp_ref[...], xn_ref[...], g1_ref[...],
                        ti > 0, ti < tiles_per_seq - 1, tm)
        h_ref[...] = h_scr[HALO:HALO + tm, :]
        a = _dot(h_scr[...], win_ref[...])
        pos = ti * tm + lax.broadcasted_iota(jnp.int32, (tm, POOL_GROUP), 0)
        for gi, w in enumerate(POOL_WINDOWS):
            cols = slice(gi * POOL_GROUP, (gi + 1) * POOL_GROUP)
            xg = a[:, cols]
            q, m = xg, 1
            while m < w // 2:
                q = q + _shift_rows(q, m)
                m *= 2
            win_sum = (q + _shift_rows(q, -(w // 2)))[HALO:HALO + tm]
            lo = jnp.maximum(pos - w // 2, 0)
            hi = jnp.minimum(pos + (w - w // 2 - 1), seq_len - 1)
            cnt = (hi - lo + 1).astype(F32)
            pooled = win_sum / cnt - xg[HALO:HALO + tm]
            y = _dot(pooled.astype(BF16), wpool_ref[gi]) * pscale_ref[:, cols]
            y_ref[:, cols] = y.astype(BF16)

    def _sgu_v(win_ref):
        gv = jax.nn.gelu(_dot(h_scr[HALO:HALO + tm, :], win_ref[...]))
        vn_scr[...] = _rmsnorm(gv, sgug_ref[...]).astype(BF16)
        n_chunk = tm // SGU_CHUNK
        for g in range(N_SGU_GROUPS):
            cols = slice(g * SGU_GROUP, (g + 1) * SGU_GROUP)
            rhs = jnp.concatenate(
                [vn_scr[c * SGU_CHUNK:(c + 1) * SGU_CHUNK, cols] for c in range(n_chunk)], axis=1)
            o = _dot(sguw_ref[g], rhs)
            for c in range(n_chunk):
                z_scr[c * SGU_CHUNK:(c + 1) * SGU_CHUNK, cols] = (
                    o[:, c * SGU_GROUP:(c + 1) * SGU_GROUP] + sgub_ref[g])

    def _sgu_u(win_ref):
        u = jax.nn.gelu(_dot(h_scr[HALO:HALO + tm, :], win_ref[...]))
        y_ref[:, D_BRANCH:2 * D_BRANCH] = (u * z_scr[...]).astype(BF16)

    def _conv_x(win_ref):
        cx_scr[...] = _dot(h_scr[...], win_ref[...])

    def _conv_c(win_ref):
        cx_scr[...] = cx_scr[...] * _dot(h_scr[...], win_ref[...])

    def _conv_b(win_ref):
        cx = cx_scr[...]
        conv = (convw_ref[0:1, :] * _shift_rows(cx, -1) + convw_ref[1:2, :] * cx
                + convw_ref[2:3, :] * _shift_rows(cx, 1))[HALO:HALO + tm]
        bg = _dot(h_scr[HALO:HALO + tm, :], win_ref[...])
        y_ref[:, 2 * D_BRANCH:] = (bg * conv).astype(BF16)

    for step, sections in enumerate(((_pool, _sgu_v, _sgu_u), (_conv_x, _conv_c, _conv_b))):
        @pl.when(s == step)
        def _step(sections=sections):
            for section, w_ref in zip(sections, (wa_ref, wb_ref, wc_ref)):
                section(w_ref)


def _mixer_in(x2d, layer, p, *, tm, seq_len):
    n_rows = x2d.shape[0]
    n_steps = len(_MIXER_IN_STEP_BLOCKS)

    def win_spec(k):
        blocks = [step_blocks[k] for step_blocks in _MIXER_IN_STEP_BLOCKS]

        def index(i, s):
            blk = jnp.int32(blocks[-1])
            for step in range(n_steps - 2, -1, -1):
                blk = jnp.where(s == step, blocks[step], blk)
            return (layer, 0, blk)
        return pl.BlockSpec((None, D_MODEL, D_BRANCH), index)

    prev_spec, next_spec = _halo_specs(tm, n_rows)
    const = lambda *shape: pl.BlockSpec((None,) + shape, lambda i, s: (layer,) + (0,) * len(shape))
    kernel = functools.partial(_mixer_in_kernel, tm=tm, tiles_per_seq=seq_len // tm, seq_len=seq_len)
    return pl.pallas_call(
        kernel,
        grid=(n_rows // tm, n_steps),
        in_specs=[
            pl.BlockSpec((tm, D_MODEL), lambda i, s: (i, 0)),
            prev_spec, next_spec,
            const(1, D_MODEL),
            win_spec(0), win_spec(1), win_spec(2),
            const(len(POOL_WINDOWS), POOL_GROUP, POOL_GROUP),
            const(1, D_BRANCH),
            const(1, D_BRANCH),
            const(N_SGU_GROUPS, SGU_CHUNK, SGU_CHUNK),
            const(N_SGU_GROUPS, SGU_CHUNK, SGU_GROUP),
            const(3, D_BRANCH),
        ],
        out_specs=[pl.BlockSpec((tm, 3 * D_BRANCH), lambda i, s: (i, 0)),
                   pl.BlockSpec((tm, D_MODEL), lambda i, s: (i, 0))],
        out_shape=[jax.ShapeDtypeStruct((n_rows, 3 * D_BRANCH), BF16),
                   jax.ShapeDtypeStruct((n_rows, D_MODEL), BF16)],
        scratch_shapes=[
            pltpu.VMEM((tm + 2 * HALO, D_MODEL), BF16),
            pltpu.VMEM((tm + 2 * HALO, D_BRANCH), F32),
            pltpu.VMEM((tm, D_BRANCH), BF16),
            pltpu.VMEM((tm, D_BRANCH), F32),
        ],
        compiler_params=pltpu.CompilerParams(
            dimension_semantics=("parallel", "arbitrary"), vmem_limit_bytes=V7X_VMEM_LIMIT_BYTES),
        name="mixer_in",
    )(x2d, x2d, x2d, p["norm1_g"], p["w_in"], p["w_in"], p["w_in"], p["w_pool"], p["pool_scale"],
      p["sgu_norm_g"],
      p["sgu_w"], p["sgu_b"], p["conv_w"])


def _mixer_out_kernel(h_ref, y_ref, wga_ref, wgb_ref, wgc_ref, wa_ref, wb_ref, wc_ref, wo_ref, out_ref, m_scr,
                      *, bn_acc):
    s = pl.program_id(1)

    def merged_block(rows):
        h = h_ref[rows, :]
        merged = (jax.nn.sigmoid(_dot(h, wga_ref[...])) * _dot(y_ref[rows, 0:D_BRANCH], wa_ref[...])
                  + jax.nn.sigmoid(_dot(h, wgb_ref[...])) * _dot(y_ref[rows, D_BRANCH:2 * D_BRANCH], wb_ref[...])
                  + jax.nn.sigmoid(_dot(h, wgc_ref[...])) * _dot(y_ref[rows, 2 * D_BRANCH:], wc_ref[...]))
        return merged.astype(BF16)

    row_tiles = [slice(r0, r0 + MXU_ACC_ROWS) for r0 in range(0, h_ref.shape[0], MXU_ACC_ROWS)]

    def park():
        for rows in row_tiles:
            m_scr[rows, :] = merged_block(rows)

    def accumulate(first):
        for rows in row_tiles:
            pair = jnp.concatenate([m_scr[rows, :], merged_block(rows)], axis=1)
            for c in range(D_MODEL // bn_acc):
                cols = slice(c * bn_acc, (c + 1) * bn_acc)
                upd = _dot(pair, wo_ref[:, cols])
                if first:
                    out_ref[rows, cols] = upd
                else:
                    out_ref[rows, cols] += upd

    pl.when(s % 2 == 0)(park)
    pl.when(s == 1)(functools.partial(accumulate, True))
    pl.when((s % 2 == 1) & (s > 1))(functools.partial(accumulate, False))


def _mixer_out(h, y, layer, p, *, tm, bn):
    n_rows = h.shape[0]
    n_steps = D_MODEL // bn

    def gate_spec(branch):
        first = (OFF_GATE + branch * D_MODEL) // bn
        return pl.BlockSpec((None, D_MODEL, bn), lambda i, s: (layer, 0, first + s))

    branch_spec = pl.BlockSpec((None, D_BRANCH, bn), lambda i, s: (layer, 0, s))
    kernel = functools.partial(_mixer_out_kernel, bn_acc=512)
    return pl.pallas_call(
        kernel,
        grid=(n_rows // tm, n_steps),
        in_specs=[
            pl.BlockSpec((tm, D_MODEL), lambda i, s: (i, 0)),
            pl.BlockSpec((tm, 3 * D_BRANCH), lambda i, s: (i, 0)),
            gate_spec(0), gate_spec(1), gate_spec(2),
            branch_spec, branch_spec, branch_spec,
            pl.BlockSpec((None, 2 * bn, D_MODEL), lambda i, s: (layer, s // 2, 0)),
        ],
        out_specs=pl.BlockSpec((tm, D_MODEL), lambda i, s: (i, 0)),
        out_shape=jax.ShapeDtypeStruct((n_rows, D_MODEL), F32),
        scratch_shapes=[pltpu.VMEM((tm, bn), BF16)],
        compiler_params=pltpu.CompilerParams(
            dimension_semantics=("parallel", "arbitrary"), vmem_limit_bytes=V7X_VMEM_LIMIT_BYTES),
        name="mixer_out",
    )(h, y, p["w_in"], p["w_in"], p["w_in"], p["w_pool_out"], p["w_sgu_out"], p["w_conv_out"], p["w_o"])


def _ffn_kernel(xm_ref, dm_ref, xe_ref, de_ref, g2_ref, wg_ref, wv_ref, cw_ref, wd_ref, gf_ref,
                out_ref, h_scr, acc_scr, cols_scr, he_scr, eg_scr, ev_scr,
                *, tm, n_tiles, tiles_per_seq, n_blocks, final_norm):
    i = pl.program_id(0)
    s = pl.program_id(1)
    ti = i % tiles_per_seq
    n_slabs = tm // SUBLANES

    n_lane_tiles = D_MODEL // LANES
    run_pitch = n_slabs + SUBLANES

    def strided_rows(start, stride):
        return jnp.concatenate(
            [cols_scr[j, pl.ds(start, SUBLANES, stride=stride), :] for j in range(n_lane_tiles)], axis=1)

    @pl.when(s == 0)
    def _init():
        for j in range(n_lane_tiles):
            cols = slice(j * LANES, (j + 1) * LANES)
            for a in range(SUBLANES):
                src = slice(a * n_slabs, (a + 1) * n_slabs)
                cols_scr[j, a * run_pitch:a * run_pitch + n_slabs, :] = xm_ref[src, cols] + dm_ref[src, cols]

        def slab(b, carry):
            acc_scr[pl.ds(pl.multiple_of(SUBLANES * b, SUBLANES), SUBLANES), :] = strided_rows(b, run_pitch)
            return carry
        lax.fori_loop(0, n_slabs, slab, 0, unroll=8)
        h_scr[...] = _rmsnorm(acc_scr[...], g2_ref[...]).astype(BF16)

    @pl.when((s == 0) & (i == 0))
    def _edge_norm():
        he_scr[...] = _rmsnorm(xe_ref[...] + de_ref[...], g2_ref[...]).astype(BF16)

    @pl.when(i == 0)
    def _edge_up():
        eg_scr[s] = _dot(he_scr[...], wg_ref[...])
        ev_scr[s] = _dot(he_scr[...], wv_ref[...])

    sub = lax.broadcasted_iota(jnp.int32, (SUBLANES, 1), 0)

    def conv3(up, edge_scr, blk):
        cw = cw_ref.at[blk]
        before = jnp.where(ti > 0, edge_scr[s, pl.ds(i, 1), :], 0.0)
        after = jnp.where(ti < tiles_per_seq - 1, edge_scr[s, pl.ds(n_tiles + i, 1), :], 0.0)
        prev0 = jnp.where(sub == 0, before, pltpu.roll(up[tm - SUBLANES:, :], 1, axis=0))
        next_last = jnp.where(sub == SUBLANES - 1, after, pltpu.roll(up[:SUBLANES, :], SUBLANES - 1, axis=0))
        prev = jnp.concatenate([prev0, up[:tm - SUBLANES, :]], axis=0)
        nxt = jnp.concatenate([up[SUBLANES:, :], next_last], axis=0)
        return cw[0:1, :] * prev + cw[1:2, :] * up + cw[2:3, :] * nxt

    h = h_scr[...]
    gate = conv3(_dot(h, wg_ref[...]), eg_scr, s)
    val = conv3(_dot(h, wv_ref[...]), ev_scr, n_blocks + s)
    act = (jax.nn.silu(gate) * val).astype(BF16)
    acc_scr[...] += _dot(act, wd_ref[...])

    @pl.when(s == n_blocks - 1)
    def _store():
        for j in range(n_lane_tiles):
            cols_scr[j, 0:tm, :] = acc_scr[:, j * LANES:(j + 1) * LANES]

        def slab(m, carry):
            per = n_slabs // SUBLANES
            rows = strided_rows(SUBLANES * SUBLANES * (m % per) + m // per, SUBLANES)
            if final_norm:
                rows = _rmsnorm(rows, gf_ref[...])
            out_ref[pl.ds(pl.multiple_of(SUBLANES * m, SUBLANES), SUBLANES), :] = rows
            return carry
        lax.fori_loop(0, n_slabs, slab, 0, unroll=8)


def _ffn(x2d, delta, layer, p, final_g, *, tm, bn, seq_len, final_norm):
    n_rows = x2d.shape[0]
    n_tiles = n_rows // tm
    n_blocks = D_FF // bn
    starts = [t * tm for t in range(n_tiles)]
    edge_rows = jnp.asarray([max(r - 1, 0) for r in starts] + [min(r + tm, n_rows - 1) for r in starts],
                            dtype=jnp.int32)
    x_edge = jnp.take(x2d, edge_rows, axis=0)
    d_edge = jnp.take(delta, edge_rows, axis=0)
    n_edge = 2 * n_tiles
    cw = p["ffn_conv_w"]
    conv_w_blocks = jnp.transpose(cw.reshape(cw.shape[0], 3, 2 * n_blocks, bn), (0, 2, 1, 3))

    main_spec = pl.BlockSpec((tm, D_MODEL), lambda i, s: (i, 0))
    edge_spec = pl.BlockSpec((n_edge, D_MODEL), lambda i, s: (0, 0))
    kernel = functools.partial(_ffn_kernel, tm=tm, n_tiles=n_tiles, tiles_per_seq=seq_len // tm,
                               n_blocks=n_blocks, final_norm=final_norm)
    return pl.pallas_call(
        kernel,
        grid=(n_tiles, n_blocks),
        in_specs=[
            main_spec, main_spec,
            edge_spec, edge_spec,
            pl.BlockSpec((None, 1, D_MODEL), lambda i, s: (layer, 0, 0)),
            pl.BlockSpec((None, D_MODEL, bn), lambda i, s: (layer, 0, s)),
            pl.BlockSpec((None, D_MODEL, bn), lambda i, s: (layer, 0, n_blocks + s)),
            pl.BlockSpec((None, 2 * n_blocks, 3, bn), lambda i, s: (layer, 0, 0, 0)),
            pl.BlockSpec((None, bn, D_MODEL), lambda i, s: (layer, s, 0)),
            pl.BlockSpec((1, D_MODEL), lambda i, s: (0, 0)),
        ],
        out_specs=main_spec,
        out_shape=jax.ShapeDtypeStruct((n_rows, D_MODEL), F32),
        scratch_shapes=[pltpu.VMEM((tm, D_MODEL), BF16),
                        pltpu.VMEM((tm, D_MODEL), F32),
                        pltpu.VMEM((D_MODEL // LANES, tm + SUBLANES * SUBLANES, LANES), F32),
                        pltpu.VMEM((n_edge, D_MODEL), BF16),
                        pltpu.VMEM((n_blocks, n_edge, bn), F32),
                        pltpu.VMEM((n_blocks, n_edge, bn), F32)],
        compiler_params=pltpu.CompilerParams(
            dimension_semantics=("arbitrary", "arbitrary"), vmem_limit_bytes=V7X_VMEM_LIMIT_BYTES),
        name="ffn",
    )(x2d, delta, x_edge, d_edge, p["norm2_g"], p["w_up"], p["w_up"], conv_w_blocks, p["w_down"], final_g)


def kernel(x, norm1_g, w_in, w_pool, pool_scale, sgu_norm_g, sgu_w, sgu_b, conv_w, w_pool_out, w_sgu_out,
           w_conv_out, w_o, norm2_g, w_up, ffn_conv_w, w_down, final_g):
    bsz, seq_len, d_model = x.shape
    depth = w_in.shape[0]
    assert d_model == D_MODEL and w_in.shape[2] == OFF_GATE + 3 * D_MODEL and w_up.shape[2] == 2 * D_FF
    tm_in, tm_out, tm_ffn = 512, 1024, 512
    assert all(seq_len % t == 0 and t % SGU_CHUNK == 0 for t in (tm_in, tm_out, tm_ffn))

    p = {
        "norm1_g": norm1_g[:, None, :], "norm2_g": norm2_g[:, None, :],
        "pool_scale": pool_scale[:, None, :], "sgu_norm_g": sgu_norm_g[:, None, :],
        "w_in": w_in.astype(BF16), "w_pool": w_pool.astype(BF16), "sgu_w": sgu_w.astype(BF16),
        "sgu_b": jnp.broadcast_to(jnp.transpose(sgu_b, (0, 2, 1))[..., None],
                                  (depth, N_SGU_GROUPS, SGU_CHUNK, SGU_GROUP)),
        "conv_w": conv_w, "ffn_conv_w": ffn_conv_w,
        "w_pool_out": w_pool_out.astype(BF16), "w_sgu_out": w_sgu_out.astype(BF16),
        "w_conv_out": w_conv_out.astype(BF16), "w_o": w_o.astype(BF16),
        "w_up": w_up.astype(BF16), "w_down": w_down.astype(BF16),
    }
    x2d = x.reshape(bsz * seq_len, d_model)
    for layer in range(depth):
        y, h = _mixer_in(x2d, layer, p, tm=tm_in, seq_len=seq_len)
        delta = _mixer_out(h, y, layer, p, tm=tm_out, bn=256)
        x2d = _ffn(x2d, delta, layer, p, final_g[None, :], tm=tm_ffn, bn=512, seq_len=seq_len,
                   final_norm=(layer == depth - 1))
    return x2d.reshape(bsz, seq_len, d_model)
```

```python
import functools

import jax
import jax.numpy as jnp
from jax import lax
from jax.experimental import pallas as pl
from jax.experimental.pallas import tpu as pltpu

D_MODEL = 2048
D_BRANCH = 1024
POOL_WINDOWS = (2, 4, 8, 16)
POOL_GROUP = D_BRANCH // len(POOL_WINDOWS)
SGU_CHUNK = 128
N_SGU_GROUPS = 8
SGU_GROUP = D_BRANCH // N_SGU_GROUPS
D_FF = 5632
EPS = 1e-6
OFF_GATE = 6 * D_BRANCH

HALO = 16
SUBLANES = 8
LANES = 128
MXU_ACC_ROWS = 512
V7X_VMEM_LIMIT_BYTES = 58 * 1024 * 1024

F32 = jnp.float32
BF16 = jnp.bfloat16


def _rmsnorm(xf, g):
    return xf * lax.rsqrt(jnp.mean(xf * xf, axis=-1, keepdims=True) + EPS) * g


def _dot(a, b):
    return jnp.dot(a, b, preferred_element_type=F32)


def _shift_rows(x, k):
    n = x.shape[0]
    return pltpu.roll(x, (-k) % n, axis=0)


def _norm_with_halo(h_scr, x_main, x_prev, x_next, g, has_prev, has_next, tm):
    h_scr[0:HALO, :] = jnp.where(has_prev, _rmsnorm(x_prev, g), 0.0).astype(BF16)
    h_scr[HALO:HALO + tm, :] = _rmsnorm(x_main, g).astype(BF16)
    h_scr[HALO + tm:, :] = jnp.where(has_next, _rmsnorm(x_next, g), 0.0).astype(BF16)


def _halo_specs(tm, n_rows):
    blocks_per_tile = tm // HALO
    last = n_rows // HALO - 1
    prev = pl.BlockSpec((HALO, D_MODEL), lambda i, s: (jnp.maximum(i * blocks_per_tile - 1, 0), 0))
    nxt = pl.BlockSpec((HALO, D_MODEL), lambda i, s: (jnp.minimum((i + 1) * blocks_per_tile, last), 0))
    return prev, nxt


_MIXER_IN_STEP_BLOCKS = ((0, 2, 1), (3, 5, 4))


def _mixer_in_kernel(xm_ref, xp_ref, xn_ref, g1_ref, wa_ref, wb_ref, wc_ref, wpool_ref, pscale_ref, sgug_ref,
                     sguw_ref, sgub_ref, convw_ref, y_ref, h_ref, h_scr, cx_scr, vn_scr, z_scr,
                     *, tm, tiles_per_seq, seq_len):
    i = pl.program_id(0)
    s = pl.program_id(1)
    ti = i % tiles_per_seq

    def _pool(win_ref):
        _norm_with_halo(h_scr, xm_ref[...], xp_ref[...], xn_ref[...], g1_ref[...],
                        ti > 0, ti < tiles_per_seq - 1, tm)
        h_ref[...] = h_scr[HALO:HALO + tm, :]
        a = _dot(h_scr[...], win_ref[...])
        pos = ti * tm + lax.broadcasted_iota(jnp.int32, (tm, POOL_GROUP), 0)
        for gi, w in enumerate(POOL_WINDOWS):
            cols = slice(gi * POOL_GROUP, (gi + 1) * POOL_GROUP)
            xg = a[:, cols]
            q, m = xg, 1
            while m < w // 2:
                q = q + _shift_rows(q, m)
                m *= 2
            win_sum = (q + _shift_rows(q, -(w // 2)))[HALO:HALO + tm]
            lo = jnp.maximum(pos - w // 2, 0)
            hi = jnp.minimum(pos + (w - w // 2 - 1), seq_len - 1)
            cnt = (hi - lo + 1).astype(F32)
            pooled = win_sum / cnt - xg[HALO:HALO + tm]
            y = _dot(pooled.astype(BF16), wpool_ref[gi]) * pscale_ref[:, cols]
            y_ref[:, cols] = y.astype(BF16)

    def _sgu_v(win_ref):
        gv = jax.nn.gelu(_dot(h_scr[HALO:HALO + tm, :], win_ref[...]))
        vn_scr[...] = _rmsnorm(gv, sgug_ref[...]).astype(BF16)
        n_chunk = tm // SGU_CHUNK
        for g in range(N_SGU_GROUPS):
            cols = slice(g * SGU_GROUP, (g + 1) * SGU_GROUP)
            rhs = jnp.concatenate(
                [vn_scr[c * SGU_CHUNK:(c + 1) * SGU_CHUNK, cols] for c in range(n_chunk)], axis=1)
            o = _dot(sguw_ref[g], rhs)
            for c in range(n_chunk):
                z_scr[c * SGU_CHUNK:(c + 1) * SGU_CHUNK, cols] = (
                    o[:, c * SGU_GROUP:(c + 1) * SGU_GROUP] + sgub_ref[g])

    def _sgu_u(win_ref):
        u = jax.nn.gelu(_dot(h_scr[HALO:HALO + tm, :], win_ref[...]))
        y_ref[:, D_BRANCH:2 * D_BRANCH] = (u * z_scr[...]).astype(BF16)

    def _conv_x(win_ref):
        cx_scr[...] = _dot(h_scr[...], win_ref[...])

    def _conv_c(win_ref):
        cx_scr[...] = cx_scr[...] * _dot(h_scr[...], win_ref[...])

    def _conv_b(win_ref):
        cx = cx_scr[...]
        conv = (convw_ref[0:1, :] * _shift_rows(cx, -1) + convw_ref[1:2, :] * cx
                + convw_ref[2:3, :] * _shift_rows(cx, 1))[HALO:HALO + tm]
        bg = _dot(h_scr[HALO:HALO + tm, :], win_ref[...])
        y_ref[:, 2 * D_BRANCH:] = (bg * conv).astype(BF16)

    for step, sections in enumerate(((_pool, _sgu_v, _sgu_u), (_conv_x, _conv_c, _conv_b))):
        @pl.when(s == step)
        def _step(sections=sections):
            for section, w_ref in zip(sections, (wa_ref, wb_ref, wc_ref)):
                section(w_ref)


def _mixer_in(x2d, layer, p, *, tm, seq_len):
    n_rows = x2d.shape[0]
    n_steps = len(_MIXER_IN_STEP_BLOCKS)

    def win_spec(k):
        blocks = [step_blocks[k] for step_blocks in _MIXER_IN_STEP_BLOCKS]

        def index(i, s):
            blk = jnp.int32(blocks[-1])
            for step in range(n_steps - 2, -1, -1):
                blk = jnp.where(s == step, blocks[step], blk)
            return (layer, 0, blk)
        return pl.BlockSpec((None, D_MODEL, D_BRANCH), index)

    prev_spec, next_spec = _halo_specs(tm, n_rows)
    const = lambda *shape: pl.BlockSpec((None,) + shape, lambda i, s: (layer,) + (0,) * len(shape))
    kernel = functools.partial(_mixer_in_kernel, tm=tm, tiles_per_seq=seq_len // tm, seq_len=seq_len)
    return pl.pallas_call(
        kernel,
        grid=(n_rows // tm, n_steps),
        in_specs=[
            pl.BlockSpec((tm, D_MODEL), lambda i, s: (i, 0)),
            prev_spec, next_spec,
            const(1, D_MODEL),
            win_spec(0), win_spec(1), win_spec(2),
            const(len(POOL_WINDOWS), POOL_GROUP, POOL_GROUP),
            const(1, D_BRANCH),
            const(1, D_BRANCH),
            const(N_SGU_GROUPS, SGU_CHUNK, SGU_CHUNK),
            const(N_SGU_GROUPS, SGU_CHUNK, SGU_GROUP),
            const(3, D_BRANCH),
        ],
        out_specs=[pl.BlockSpec((tm, 3 * D_BRANCH), lambda i, s: (i, 0)),
                   pl.BlockSpec((tm, D_MODEL), lambda i, s: (i, 0))],
        out_shape=[jax.ShapeDtypeStruct((n_rows, 3 * D_BRANCH), BF16),
                   jax.ShapeDtypeStruct((n_rows, D_MODEL), BF16)],
        scratch_shapes=[
            pltpu.VMEM((tm + 2 * HALO, D_MODEL), BF16),
            pltpu.VMEM((tm + 2 * HALO, D_BRANCH), F32),
            pltpu.VMEM((tm, D_BRANCH), BF16),
            pltpu.VMEM((tm, D_BRANCH), F32),
        ],
        compiler_params=pltpu.CompilerParams(
            dimension_semantics=("parallel", "arbitrary"), vmem_limit_bytes=V7X_VMEM_LIMIT_BYTES),
        name="mixer_in",
    )(x2d, x2d, x2d, p["norm1_g"], p["w_in"], p["w_in"], p["w_in"], p["w_pool"], p["pool_scale"],
      p["sgu_norm_g"],
      p["sgu_w"], p["sgu_b"], p["conv_w"])


def _mixer_out_kernel(h_ref, y_ref, wga_ref, wgb_ref, wgc_ref, wa_ref, wb_ref, wc_ref, wo_ref, out_ref, m_scr,
                      *, bn_acc):
    s = pl.program_id(1)

    def merged_block(rows):
        h = h_ref[rows, :]
        merged = (jax.nn.sigmoid(_dot(h, wga_ref[...])) * _dot(y_ref[rows, 0:D_BRANCH], wa_ref[...])
                  + jax.nn.sigmoid(_dot(h, wgb_ref[...])) * _dot(y_ref[rows, D_BRANCH:2 * D_BRANCH], wb_ref[...])
                  + jax.nn.sigmoid(_dot(h, wgc_ref[...])) * _dot(y_ref[rows, 2 * D_BRANCH:], wc_ref[...]))
        return merged.astype(BF16)

    row_tiles = [slice(r0, r0 + MXU_ACC_ROWS) for r0 in range(0, h_ref.shape[0], MXU_ACC_ROWS)]

    group = m_scr.shape[0] + 1

    def park(slot):
        for rows in row_tiles:
            m_scr[slot, rows, :] = merged_block(rows)

    def accumulate(first):
        for rows in row_tiles:
            blocks = jnp.concatenate([m_scr[k, rows, :] for k in range(group - 1)] + [merged_block(rows)], axis=1)
            for c in range(D_MODEL // bn_acc):
                cols = slice(c * bn_acc, (c + 1) * bn_acc)
                upd = _dot(blocks, wo_ref[:, cols])
                if first:
                    out_ref[rows, cols] = upd
                else:
                    out_ref[rows, cols] += upd

    for slot in range(group - 1):
        pl.when(s % group == slot)(functools.partial(park, slot))
    pl.when(s == group - 1)(functools.partial(accumulate, True))
    pl.when((s % group == group - 1) & (s >= group))(functools.partial(accumulate, False))


def _mixer_out(h, y, layer, p, *, tm, bn, group):
    n_rows = h.shape[0]
    n_steps = D_MODEL // bn
    assert n_steps % group == 0

    def gate_spec(branch):
        first = (OFF_GATE + branch * D_MODEL) // bn
        return pl.BlockSpec((None, D_MODEL, bn), lambda i, s: (layer, 0, first + s))

    branch_spec = pl.BlockSpec((None, D_BRANCH, bn), lambda i, s: (layer, 0, s))
    kernel = functools.partial(_mixer_out_kernel, bn_acc=512)
    return pl.pallas_call(
        kernel,
        grid=(n_rows // tm, n_steps),
        in_specs=[
            pl.BlockSpec((tm, D_MODEL), lambda i, s: (i, 0)),
            pl.BlockSpec((tm, 3 * D_BRANCH), lambda i, s: (i, 0)),
            gate_spec(0), gate_spec(1), gate_spec(2),
            branch_spec, branch_spec, branch_spec,
            pl.BlockSpec((None, group * bn, D_MODEL), lambda i, s: (layer, s // group, 0)),
        ],
        out_specs=pl.BlockSpec((tm, D_MODEL), lambda i, s: (i, 0)),
        out_shape=jax.ShapeDtypeStruct((n_rows, D_MODEL), F32),
        scratch_shapes=[pltpu.VMEM((group - 1, tm, bn), BF16)],
        compiler_params=pltpu.CompilerParams(
            dimension_semantics=("parallel", "arbitrary"), vmem_limit_bytes=V7X_VMEM_LIMIT_BYTES),
        name="mixer_out",
    )(h, y, p["w_in"], p["w_in"], p["w_in"], p["w_pool_out"], p["w_sgu_out"], p["w_conv_out"], p["w_o"])


def _ffn_kernel(xm_ref, dm_ref, xe_ref, de_ref, g2_ref, wg_ref, wv_ref, cw_ref, wd_ref, gf_ref,
                out_ref, h_scr, acc_scr, cols_scr, he_scr, eg_scr, ev_scr,
                *, tm, n_tiles, tiles_per_seq, n_blocks, final_norm):
    i = pl.program_id(0)
    s = pl.program_id(1)
    ti = i % tiles_per_seq
    n_slabs = tm // SUBLANES

    n_lane_tiles = D_MODEL // LANES
    run_pitch = n_slabs + SUBLANES

    def strided_rows(start, stride):
        return jnp.concatenate(
            [cols_scr[j, pl.ds(start, SUBLANES, stride=stride), :] for j in range(n_lane_tiles)], axis=1)

    @pl.when(s == 0)
    def _init():
        for j in range(n_lane_tiles):
            cols = slice(j * LANES, (j + 1) * LANES)
            for a in range(SUBLANES):
                src = slice(a * n_slabs, (a + 1) * n_slabs)
                cols_scr[j, a * run_pitch:a * run_pitch + n_slabs, :] = xm_ref[src, cols] + dm_ref[src, cols]

        def slab(b, carry):
            acc_scr[pl.ds(pl.multiple_of(SUBLANES * b, SUBLANES), SUBLANES), :] = strided_rows(b, run_pitch)
            return carry
        lax.fori_loop(0, n_slabs, slab, 0, unroll=8)
        h_scr[...] = _rmsnorm(acc_scr[...], g2_ref[...]).astype(BF16)

    @pl.when((s == 0) & (i == 0))
    def _edge_norm():
        he_scr[...] = _rmsnorm(xe_ref[...] + de_ref[...], g2_ref[...]).astype(BF16)

    @pl.when(i == 0)
    def _edge_up():
        eg_scr[s] = _dot(he_scr[...], wg_ref[...])
        ev_scr[s] = _dot(he_scr[...], wv_ref[...])

    sub = lax.broadcasted_iota(jnp.int32, (SUBLANES, 1), 0)

    def conv3(up, edge_scr, blk):
        cw = cw_ref.at[blk]
        before = jnp.where(ti > 0, edge_scr[s, pl.ds(i, 1), :], 0.0)
        after = jnp.where(ti < tiles_per_seq - 1, edge_scr[s, pl.ds(n_tiles + i, 1), :], 0.0)
        prev0 = jnp.where(sub == 0, before, pltpu.roll(up[tm - SUBLANES:, :], 1, axis=0))
        next_last = jnp.where(sub == SUBLANES - 1, after, pltpu.roll(up[:SUBLANES, :], SUBLANES - 1, axis=0))
        prev = jnp.concatenate([prev0, up[:tm - SUBLANES, :]], axis=0)
        nxt = jnp.concatenate([up[SUBLANES:, :], next_last], axis=0)
        return cw[0:1, :] * prev + cw[1:2, :] * up + cw[2:3, :] * nxt

    h = h_scr[...]
    gate = conv3(_dot(h, wg_ref[...]), eg_scr, s)
    val = conv3(_dot(h, wv_ref[...]), ev_scr, n_blocks + s)
    act = (jax.nn.silu(gate) * val).astype(BF16)
    acc_scr[...] += _dot(act, wd_ref[...])

    @pl.when(s == n_blocks - 1)
    def _store():
        for j in range(n_lane_tiles):
            cols_scr[j, 0:tm, :] = acc_scr[:, j * LANES:(j + 1) * LANES]

        def slab(m, carry):
            per = n_slabs // SUBLANES
            rows = strided_rows(SUBLANES * SUBLANES * (m % per) + m // per, SUBLANES)
            if final_norm:
                rows = _rmsnorm(rows, gf_ref[...])
            out_ref[pl.ds(pl.multiple_of(SUBLANES * m, SUBLANES), SUBLANES), :] = rows
            return carry
        lax.fori_loop(0, n_slabs, slab, 0, unroll=8)


def _ffn(x2d, delta, layer, p, final_g, *, tm, bn, seq_len, final_norm):
    n_rows = x2d.shape[0]
    n_tiles = n_rows // tm
    n_blocks = D_FF // bn
    starts = [t * tm for t in range(n_tiles)]
    edge_rows = jnp.asarray([max(r - 1, 0) for r in starts] + [min(r + tm, n_rows - 1) for r in starts],
                            dtype=jnp.int32)
    x_edge = jnp.take(x2d, edge_rows, axis=0)
    d_edge = jnp.take(delta, edge_rows, axis=0)
    n_edge = 2 * n_tiles
    cw = p["ffn_conv_w"]
    conv_w_blocks = jnp.transpose(cw.reshape(cw.shape[0], 3, 2 * n_blocks, bn), (0, 2, 1, 3))

    main_spec = pl.BlockSpec((tm, D_MODEL), lambda i, s: (i, 0))
    edge_spec = pl.BlockSpec((n_edge, D_MODEL), lambda i, s: (0, 0))
    kernel = functools.partial(_ffn_kernel, tm=tm, n_tiles=n_tiles, tiles_per_seq=seq_len // tm,
                               n_blocks=n_blocks, final_norm=final_norm)
    return pl.pallas_call(
        kernel,
        grid=(n_tiles, n_blocks),
        in_specs=[
            main_spec, main_spec,
            edge_spec, edge_spec,
            pl.BlockSpec((None, 1, D_MODEL), lambda i, s: (layer, 0, 0)),
            pl.BlockSpec((None, D_MODEL, bn), lambda i, s: (layer, 0, s)),
            pl.BlockSpec((None, D_MODEL, bn), lambda i, s: (layer, 0, n_blocks + s)),
            pl.BlockSpec((None, 2 * n_blocks, 3, bn), lambda i, s: (layer, 0, 0, 0)),
            pl.BlockSpec((None, bn, D_MODEL), lambda i, s: (layer, s, 0)),
            pl.BlockSpec((1, D_MODEL), lambda i, s: (0, 0)),
        ],
        out_specs=main_spec,
        out_shape=jax.ShapeDtypeStruct((n_rows, D_MODEL), F32),
        scratch_shapes=[pltpu.VMEM((tm, D_MODEL), BF16),
                        pltpu.VMEM((tm, D_MODEL), F32),
                        pltpu.VMEM((D_MODEL // LANES, tm + SUBLANES * SUBLANES, LANES), F32),
                        pltpu.VMEM((n_edge, D_MODEL), BF16),
                        pltpu.VMEM((n_blocks, n_edge, bn), F32),
                        pltpu.VMEM((n_blocks, n_edge, bn), F32)],
        compiler_params=pltpu.CompilerParams(
            dimension_semantics=("arbitrary", "arbitrary"), vmem_limit_bytes=V7X_VMEM_LIMIT_BYTES),
        name="ffn",
    )(x2d, delta, x_edge, d_edge, p["norm2_g"], p["w_up"], p["w_up"], conv_w_blocks, p["w_down"], final_g)


def kernel(x, norm1_g, w_in, w_pool, pool_scale, sgu_norm_g, sgu_w, sgu_b, conv_w, w_pool_out, w_sgu_out,
           w_conv_out, w_o, norm2_g, w_up, ffn_conv_w, w_down, final_g):
    bsz, seq_len, d_model = x.shape
    depth = w_in.shape[0]
    assert d_model == D_MODEL and w_in.shape[2] == OFF_GATE + 3 * D_MODEL and w_up.shape[2] == 2 * D_FF
    tm_in, tm_out, tm_ffn = 512, 1024, 512
    assert all(seq_len % t == 0 and t % SGU_CHUNK == 0 for t in (tm_in, tm_out, tm_ffn))

    p = {
        "norm1_g": norm1_g[:, None, :], "norm2_g": norm2_g[:, None, :],
        "pool_scale": pool_scale[:, None, :], "sgu_norm_g": sgu_norm_g[:, None, :],
        "w_in": w_in.astype(BF16), "w_pool": w_pool.astype(BF16), "sgu_w": sgu_w.astype(BF16),
        "sgu_b": jnp.broadcast_to(jnp.transpose(sgu_b, (0, 2, 1))[..., None],
                                  (depth, N_SGU_GROUPS, SGU_CHUNK, SGU_GROUP)),
        "conv_w": conv_w, "ffn_conv_w": ffn_conv_w,
        "w_pool_out": w_pool_out.astype(BF16), "w_sgu_out": w_sgu_out.astype(BF16),
        "w_conv_out": w_conv_out.astype(BF16), "w_o": w_o.astype(BF16),
        "w_up": w_up.astype(BF16), "w_down": w_down.astype(BF16),
    }
    x2d = x.reshape(bsz * seq_len, d_model)
    for layer in range(depth):
        y, h = _mixer_in(x2d, layer, p, tm=tm_in, seq_len=seq_len)
        delta = _mixer_out(h, y, layer, p, tm=tm_out, bn=256, group=4)
        x2d = _ffn(x2d, delta, layer, p, final_g[None, :], tm=tm_ffn, bn=512, seq_len=seq_len,
                   final_norm=(layer == depth - 1))
    return x2d.reshape(bsz, seq_len, d_model)
```

```python
import functools

import jax
import jax.numpy as jnp
from jax import lax
from jax.experimental import pallas as pl
from jax.experimental.pallas import tpu as pltpu

D_MODEL = 2048
D_BRANCH = 1024
POOL_WINDOWS = (2, 4, 8, 16)
POOL_GROUP = D_BRANCH // len(POOL_WINDOWS)
SGU_CHUNK = 128
N_SGU_GROUPS = 8
SGU_GROUP = D_BRANCH // N_SGU_GROUPS
D_FF = 5632
EPS = 1e-6
OFF_GATE = 6 * D_BRANCH

HALO = 16
SUBLANES = 8
LANES = 128
MXU_ACC_ROWS = 512
V7X_VMEM_LIMIT_BYTES = 58 * 1024 * 1024

F32 = jnp.float32
BF16 = jnp.bfloat16


def _rmsnorm(xf, g):
    return xf * lax.rsqrt(jnp.mean(xf * xf, axis=-1, keepdims=True) + EPS) * g


def _dot(a, b):
    return jnp.dot(a, b, preferred_element_type=F32)


def _shift_rows(x, k):
    n = x.shape[0]
    return pltpu.roll(x, (-k) % n, axis=0)


def _norm_with_halo(h_scr, x_main, x_prev, x_next, g, has_prev, has_next, tm):
    h_scr[0:HALO, :] = jnp.where(has_prev, _rmsnorm(x_prev, g), 0.0).astype(BF16)
    h_scr[HALO:HALO + tm, :] = _rmsnorm(x_main, g).astype(BF16)
    h_scr[HALO + tm:, :] = jnp.where(has_next, _rmsnorm(x_next, g), 0.0).astype(BF16)


def _halo_specs(tm, n_rows):
    blocks_per_tile = tm // HALO
    last = n_rows // HALO - 1
    prev = pl.BlockSpec((HALO, D_MODEL), lambda i, s: (jnp.maximum(i * blocks_per_tile - 1, 0), 0))
    nxt = pl.BlockSpec((HALO, D_MODEL), lambda i, s: (jnp.minimum((i + 1) * blocks_per_tile, last), 0))
    return prev, nxt


_MIXER_IN_SECTIONS = (("a", 0), ("v", 2), ("u", 1), ("x_c", 3), ("C", 5), ("B", 4))


def _mixer_in_kernel(xm_ref, xp_ref, xn_ref, g1_ref, w_ref, wpool_ref, pscale_ref, sgug_ref,
                     sguw_ref, sgub_ref, convw_ref, y_ref, h_ref, h_scr, cx_scr, vn_scr, z_scr,
                     *, tm, tiles_per_seq, seq_len):
    i = pl.program_id(0)
    ti = i % tiles_per_seq

    def _pool(win_ref):
        _norm_with_halo(h_scr, xm_ref[...], xp_ref[...], xn_ref[...], g1_ref[...],
                        ti > 0, ti < tiles_per_seq - 1, tm)
        h_ref[...] = h_scr[HALO:HALO + tm, :]
        a = _dot(h_scr[...], win_ref[...])
        pos = ti * tm + lax.broadcasted_iota(jnp.int32, (tm, POOL_GROUP), 0)
        for gi, w in enumerate(POOL_WINDOWS):
            cols = slice(gi * POOL_GROUP, (gi + 1) * POOL_GROUP)
            xg = a[:, cols]
            q, m = xg, 1
            while m < w // 2:
                q = q + _shift_rows(q, m)
                m *= 2
            win_sum = (q + _shift_rows(q, -(w // 2)))[HALO:HALO + tm]
            lo = jnp.maximum(pos - w // 2, 0)
            hi = jnp.minimum(pos + (w - w // 2 - 1), seq_len - 1)
            cnt = (hi - lo + 1).astype(F32)
            pooled = win_sum / cnt - xg[HALO:HALO + tm]
            y = _dot(pooled.astype(BF16), wpool_ref[gi]) * pscale_ref[:, cols]
            y_ref[:, cols] = y.astype(BF16)

    def _sgu_v(win_ref):
        gv = jax.nn.gelu(_dot(h_scr[HALO:HALO + tm, :], win_ref[...]))
        vn_scr[...] = _rmsnorm(gv, sgug_ref[...]).astype(BF16)
        n_chunk = tm // SGU_CHUNK
        for g in range(N_SGU_GROUPS):
            cols = slice(g * SGU_GROUP, (g + 1) * SGU_GROUP)
            rhs = jnp.concatenate(
                [vn_scr[c * SGU_CHUNK:(c + 1) * SGU_CHUNK, cols] for c in range(n_chunk)], axis=1)
            o = _dot(sguw_ref[g], rhs)
            for c in range(n_chunk):
                z_scr[c * SGU_CHUNK:(c + 1) * SGU_CHUNK, cols] = (
                    o[:, c * SGU_GROUP:(c + 1) * SGU_GROUP] + sgub_ref[g])

    def _sgu_u(win_ref):
        u = jax.nn.gelu(_dot(h_scr[HALO:HALO + tm, :], win_ref[...]))
        y_ref[:, D_BRANCH:2 * D_BRANCH] = (u * z_scr[...]).astype(BF16)

    def _conv_x(win_ref):
        cx_scr[...] = _dot(h_scr[...], win_ref[...])

    def _conv_c(win_ref):
        cx_scr[...] = cx_scr[...] * _dot(h_scr[...], win_ref[...])

    def _conv_b(win_ref):
        cx = cx_scr[...]
        conv = (convw_ref[0:1, :] * _shift_rows(cx, -1) + convw_ref[1:2, :] * cx
                + convw_ref[2:3, :] * _shift_rows(cx, 1))[HALO:HALO + tm]
        bg = _dot(h_scr[HALO:HALO + tm, :], win_ref[...])
        y_ref[:, 2 * D_BRANCH:] = (bg * conv).astype(BF16)

    sections = {"a": _pool, "v": _sgu_v, "u": _sgu_u, "x_c": _conv_x, "C": _conv_c, "B": _conv_b}
    for name, blk in _MIXER_IN_SECTIONS:
        sections[name](w_ref.at[:, blk * D_BRANCH:(blk + 1) * D_BRANCH])


def _mixer_in(x2d, layer, p, *, tm, seq_len):
    n_rows = x2d.shape[0]
    prev_spec, next_spec = _halo_specs(tm, n_rows)
    const = lambda *shape: pl.BlockSpec((None,) + shape, lambda i, s: (layer,) + (0,) * len(shape))
    kernel = functools.partial(_mixer_in_kernel, tm=tm, tiles_per_seq=seq_len // tm, seq_len=seq_len)
    return pl.pallas_call(
        kernel,
        grid=(n_rows // tm, 1),
        in_specs=[
            pl.BlockSpec((tm, D_MODEL), lambda i, s: (i, 0)),
            prev_spec, next_spec,
            const(1, D_MODEL),
            pl.BlockSpec((None, D_MODEL, OFF_GATE), lambda i, s: (layer, 0, 0), pipeline_mode=pl.Buffered(1)),
            const(len(POOL_WINDOWS), POOL_GROUP, POOL_GROUP),
            const(1, D_BRANCH),
            const(1, D_BRANCH),
            const(N_SGU_GROUPS, SGU_CHUNK, SGU_CHUNK),
            const(N_SGU_GROUPS, SGU_CHUNK, SGU_GROUP),
            const(3, D_BRANCH),
        ],
        out_specs=[pl.BlockSpec((tm, 3 * D_BRANCH), lambda i, s: (i, 0)),
                   pl.BlockSpec((tm, D_MODEL), lambda i, s: (i, 0))],
        out_shape=[jax.ShapeDtypeStruct((n_rows, 3 * D_BRANCH), BF16),
                   jax.ShapeDtypeStruct((n_rows, D_MODEL), BF16)],
        scratch_shapes=[
            pltpu.VMEM((tm + 2 * HALO, D_MODEL), BF16),
            pltpu.VMEM((tm + 2 * HALO, D_BRANCH), F32),
            pltpu.VMEM((tm, D_BRANCH), BF16),
            pltpu.VMEM((tm, D_BRANCH), F32),
        ],
        compiler_params=pltpu.CompilerParams(
            dimension_semantics=("parallel", "arbitrary"), vmem_limit_bytes=V7X_VMEM_LIMIT_BYTES),
        name="mixer_in",
    )(x2d, x2d, x2d, p["norm1_g"], p["w_in"], p["w_pool"], p["pool_scale"], p["sgu_norm_g"],
      p["sgu_w"], p["sgu_b"], p["conv_w"])


def _mixer_out_kernel(h_ref, y_ref, wga_ref, wgb_ref, wgc_ref, wa_ref, wb_ref, wc_ref, wo_ref, out_ref, m_scr,
                      *, bn_acc):
    s = pl.program_id(1)

    def merged_block(rows):
        h = h_ref[rows, :]
        merged = (jax.nn.sigmoid(_dot(h, wga_ref[...])) * _dot(y_ref[rows, 0:D_BRANCH], wa_ref[...])
                  + jax.nn.sigmoid(_dot(h, wgb_ref[...])) * _dot(y_ref[rows, D_BRANCH:2 * D_BRANCH], wb_ref[...])
                  + jax.nn.sigmoid(_dot(h, wgc_ref[...])) * _dot(y_ref[rows, 2 * D_BRANCH:], wc_ref[...]))
        return merged.astype(BF16)

    row_tiles = [slice(r0, r0 + MXU_ACC_ROWS) for r0 in range(0, h_ref.shape[0], MXU_ACC_ROWS)]

    group = m_scr.shape[0] + 1

    def park(slot):
        for rows in row_tiles:
            m_scr[slot, rows, :] = merged_block(rows)

    def accumulate(first):
        for rows in row_tiles:
            blocks = jnp.concatenate([m_scr[k, rows, :] for k in range(group - 1)] + [merged_block(rows)], axis=1)
            for c in range(D_MODEL // bn_acc):
                cols = slice(c * bn_acc, (c + 1) * bn_acc)
                upd = _dot(blocks, wo_ref[:, cols])
                if first:
                    out_ref[rows, cols] = upd
                else:
                    out_ref[rows, cols] += upd

    for slot in range(group - 1):
        pl.when(s % group == slot)(functools.partial(park, slot))
    pl.when(s == group - 1)(functools.partial(accumulate, True))
    pl.when((s % group == group - 1) & (s >= group))(functools.partial(accumulate, False))


def _mixer_out(h, y, layer, p, *, tm, bn, group):
    n_rows = h.shape[0]
    n_steps = D_MODEL // bn
    assert n_steps % group == 0

    def gate_spec(branch):
        first = (OFF_GATE + branch * D_MODEL) // bn
        return pl.BlockSpec((None, D_MODEL, bn), lambda i, s: (layer, 0, first + s))

    branch_spec = pl.BlockSpec((None, D_BRANCH, bn), lambda i, s: (layer, 0, s))
    kernel = functools.partial(_mixer_out_kernel, bn_acc=512)
    return pl.pallas_call(
        kernel,
        grid=(n_rows // tm, n_steps),
        in_specs=[
            pl.BlockSpec((tm, D_MODEL), lambda i, s: (i, 0)),
            pl.BlockSpec((tm, 3 * D_BRANCH), lambda i, s: (i, 0)),
            gate_spec(0), gate_spec(1), gate_spec(2),
            branch_spec, branch_spec, branch_spec,
            pl.BlockSpec((None, group * bn, D_MODEL), lambda i, s: (layer, s // group, 0)),
        ],
        out_specs=pl.BlockSpec((tm, D_MODEL), lambda i, s: (i, 0)),
        out_shape=jax.ShapeDtypeStruct((n_rows, D_MODEL), F32),
        scratch_shapes=[pltpu.VMEM((group - 1, tm, bn), BF16)],
        compiler_params=pltpu.CompilerParams(
            dimension_semantics=("parallel", "arbitrary"), vmem_limit_bytes=V7X_VMEM_LIMIT_BYTES),
        name="mixer_out",
    )(h, y, p["w_in"], p["w_in"], p["w_in"], p["w_pool_out"], p["w_sgu_out"], p["w_conv_out"], p["w_o"])


def _ffn_kernel(xm_ref, dm_ref, xe_ref, de_ref, g2_ref, wg_ref, wv_ref, cw_ref, wd_ref, gf_ref,
                out_ref, h_scr, acc_scr, cols_scr, he_scr, eg_scr, ev_scr,
                *, tm, n_tiles, tiles_per_seq, n_blocks, final_norm):
    i = pl.program_id(0)
    s = pl.program_id(1)
    ti = i % tiles_per_seq
    n_slabs = tm // SUBLANES

    n_lane_tiles = D_MODEL // LANES
    run_pitch = n_slabs + SUBLANES

    def strided_rows(start, stride):
        return jnp.concatenate(
            [cols_scr[j, pl.ds(start, SUBLANES, stride=stride), :] for j in range(n_lane_tiles)], axis=1)

    @pl.when(s == 0)
    def _init():
        for j in range(n_lane_tiles):
            cols = slice(j * LANES, (j + 1) * LANES)
            for a in range(SUBLANES):
                src = slice(a * n_slabs, (a + 1) * n_slabs)
                cols_scr[j, a * run_pitch:a * run_pitch + n_slabs, :] = xm_ref[src, cols] + dm_ref[src, cols]

        def slab(b, carry):
            acc_scr[pl.ds(pl.multiple_of(SUBLANES * b, SUBLANES), SUBLANES), :] = strided_rows(b, run_pitch)
            return carry
        lax.fori_loop(0, n_slabs, slab, 0, unroll=8)
        h_scr[...] = _rmsnorm(acc_scr[...], g2_ref[...]).astype(BF16)

    @pl.when((s == 0) & (i == 0))
    def _edge_norm():
        he_scr[...] = _rmsnorm(xe_ref[...] + de_ref[...], g2_ref[...]).astype(BF16)

    @pl.when(i == 0)
    def _edge_up():
        eg_scr[s] = _dot(he_scr[...], wg_ref[...])
        ev_scr[s] = _dot(he_scr[...], wv_ref[...])

    sub = lax.broadcasted_iota(jnp.int32, (SUBLANES, 1), 0)

    def conv3(up, edge_scr, blk):
        cw = cw_ref.at[blk]
        before = jnp.where(ti > 0, edge_scr[s, pl.ds(i, 1), :], 0.0)
        after = jnp.where(ti < tiles_per_seq - 1, edge_scr[s, pl.ds(n_tiles + i, 1), :], 0.0)
        prev0 = jnp.where(sub == 0, before, pltpu.roll(up[tm - SUBLANES:, :], 1, axis=0))
        next_last = jnp.where(sub == SUBLANES - 1, after, pltpu.roll(up[:SUBLANES, :], SUBLANES - 1, axis=0))
        prev = jnp.concatenate([prev0, up[:tm - SUBLANES, :]], axis=0)
        nxt = jnp.concatenate([up[SUBLANES:, :], next_last], axis=0)
        return cw[0:1, :] * prev + cw[1:2, :] * up + cw[2:3, :] * nxt

    h = h_scr[...]
    gate = conv3(_dot(h, wg_ref[...]), eg_scr, s)
    val = conv3(_dot(h, wv_ref[...]), ev_scr, n_blocks + s)
    act = (jax.nn.silu(gate) * val).astype(BF16)
    acc_scr[...] += _dot(act, wd_ref[...])

    @pl.when(s == n_blocks - 1)
    def _store():
        for j in range(n_lane_tiles):
            cols_scr[j, 0:tm, :] = acc_scr[:, j * LANES:(j + 1) * LANES]

        def slab(m, carry):
            per = n_slabs // SUBLANES
            rows = strided_rows(SUBLANES * SUBLANES * (m % per) + m // per, SUBLANES)
            if final_norm:
                rows = _rmsnorm(rows, gf_ref[...])
            out_ref[pl.ds(pl.multiple_of(SUBLANES * m, SUBLANES), SUBLANES), :] = rows
            return carry
        lax.fori_loop(0, n_slabs, slab, 0, unroll=8)


def _ffn(x2d, delta, layer, p, final_g, *, tm, bn, seq_len, final_norm):
    n_rows = x2d.shape[0]
    n_tiles = n_rows // tm
    n_blocks = D_FF // bn
    starts = [t * tm for t in range(n_tiles)]
    edge_rows = jnp.asarray([max(r - 1, 0) for r in starts] + [min(r + tm, n_rows - 1) for r in starts],
                            dtype=jnp.int32)
    x_edge = jnp.take(x2d, edge_rows, axis=0)
    d_edge = jnp.take(delta, edge_rows, axis=0)
    n_edge = 2 * n_tiles
    cw = p["ffn_conv_w"]
    conv_w_blocks = jnp.transpose(cw.reshape(cw.shape[0], 3, 2 * n_blocks, bn), (0, 2, 1, 3))

    main_spec = pl.BlockSpec((tm, D_MODEL), lambda i, s: (i, 0))
    edge_spec = pl.BlockSpec((n_edge, D_MODEL), lambda i, s: (0, 0))
    kernel = functools.partial(_ffn_kernel, tm=tm, n_tiles=n_tiles, tiles_per_seq=seq_len // tm,
                               n_blocks=n_blocks, final_norm=final_norm)
    return pl.pallas_call(
        kernel,
        grid=(n_tiles, n_blocks),
        in_specs=[
            main_spec, main_spec,
            edge_spec, edge_spec,
            pl.BlockSpec((None, 1, D_MODEL), lambda i, s: (layer, 0, 0)),
            pl.BlockSpec((None, D_MODEL, bn), lambda i, s: (layer, 0, s)),
            pl.BlockSpec((None, D_MODEL, bn), lambda i, s: (layer, 0, n_blocks + s)),
            pl.BlockSpec((None, 2 * n_blocks, 3, bn), lambda i, s: (layer, 0, 0, 0)),
            pl.BlockSpec((None, bn, D_MODEL), lambda i, s: (layer, s, 0)),
            pl.BlockSpec((1, D_MODEL), lambda i, s: (0, 0)),
        ],
        out_specs=main_spec,
        out_shape=jax.ShapeDtypeStruct((n_rows, D_MODEL), F32),
        scratch_shapes=[pltpu.VMEM((tm, D_MODEL), BF16),
                        pltpu.VMEM((tm, D_MODEL), F32),
                        pltpu.VMEM((D_MODEL // LANES, tm + SUBLANES * SUBLANES, LANES), F32),
                        pltpu.VMEM((n_edge, D_MODEL), BF16),
                        pltpu.VMEM((n_blocks, n_edge, bn), F32),
                        pltpu.VMEM((n_blocks, n_edge, bn), F32)],
        compiler_params=pltpu.CompilerParams(
            dimension_semantics=("arbitrary", "arbitrary"), vmem_limit_bytes=V7X_VMEM_LIMIT_BYTES),
        name="ffn",
    )(x2d, delta, x_edge, d_edge, p["norm2_g"], p["w_up"], p["w_up"], conv_w_blocks, p["w_down"], final_g)


def kernel(x, norm1_g, w_in, w_pool, pool_scale, sgu_norm_g, sgu_w, sgu_b, conv_w, w_pool_out, w_sgu_out,
           w_conv_out, w_o, norm2_g, w_up, ffn_conv_w, w_down, final_g):
    bsz, seq_len, d_model = x.shape
    depth = w_in.shape[0]
    assert d_model == D_MODEL and w_in.shape[2] == OFF_GATE + 3 * D_MODEL and w_up.shape[2] == 2 * D_FF
    tm_in, tm_out, tm_ffn = 512, 1024, 512
    assert all(seq_len % t == 0 and t % SGU_CHUNK == 0 for t in (tm_in, tm_out, tm_ffn))

    p = {
        "norm1_g": norm1_g[:, None, :], "norm2_g": norm2_g[:, None, :],
        "pool_scale": pool_scale[:, None, :], "sgu_norm_g": sgu_norm_g[:, None, :],
        "w_in": w_in.astype(BF16), "w_pool": w_pool.astype(BF16), "sgu_w": sgu_w.astype(BF16),
        "sgu_b": jnp.broadcast_to(jnp.transpose(sgu_b, (0, 2, 1))[..., None],
                                  (depth, N_SGU_GROUPS, SGU_CHUNK, SGU_GROUP)),
        "conv_w": conv_w, "ffn_conv_w": ffn_conv_w,
        "w_pool_out": w_pool_out.astype(BF16), "w_sgu_out": w_sgu_out.astype(BF16),
        "w_conv_out": w_conv_out.astype(BF16), "w_o": w_o.astype(BF16),
        "w_up": w_up.astype(BF16), "w_down": w_down.astype(BF16),
    }
    x2d = x.reshape(bsz * seq_len, d_model)
    for layer in range(depth):
        y, h = _mixer_in(x2d, layer, p, tm=tm_in, seq_len=seq_len)
        delta = _mixer_out(h, y, layer, p, tm=tm_out, bn=256, group=4)
        x2d = _ffn(x2d, delta, layer, p, final_g[None, :], tm=tm_ffn, bn=512, seq_len=seq_len,
                   final_norm=(layer == depth - 1))
    return x2d.reshape(bsz, seq_len, d_model)
```

```python
import functools

import jax
import jax.numpy as jnp
from jax import lax
from jax.experimental import pallas as pl
from jax.experimental.pallas import tpu as pltpu

D_MODEL = 2048
D_BRANCH = 1024
POOL_WINDOWS = (2, 4, 8, 16)
POOL_GROUP = D_BRANCH // len(POOL_WINDOWS)
SGU_CHUNK = 128
N_SGU_GROUPS = 8
SGU_GROUP = D_BRANCH // N_SGU_GROUPS
D_FF = 5632
EPS = 1e-6
OFF_GATE = 6 * D_BRANCH

HALO = 16
SUBLANES = 8
LANES = 128
MXU_ACC_ROWS = 512
V7X_VMEM_LIMIT_BYTES = 58 * 1024 * 1024
V7X_VMEM_LIMIT_FFN_BYTES = 63 * 1024 * 1024

F32 = jnp.float32
BF16 = jnp.bfloat16


def _rmsnorm(xf, g):
    return xf * lax.rsqrt(jnp.mean(xf * xf, axis=-1, keepdims=True) + EPS) * g


def _dot(a, b):
    return jnp.dot(a, b, preferred_element_type=F32)


def _shift_rows(x, k):
    n = x.shape[0]
    return pltpu.roll(x, (-k) % n, axis=0)


def _norm_with_halo(h_scr, x_main, x_prev, x_next, g, has_prev, has_next, tm):
    h_scr[0:HALO, :] = jnp.where(has_prev, _rmsnorm(x_prev, g), 0.0).astype(BF16)
    h_scr[HALO:HALO + tm, :] = _rmsnorm(x_main, g).astype(BF16)
    h_scr[HALO + tm:, :] = jnp.where(has_next, _rmsnorm(x_next, g), 0.0).astype(BF16)


def _halo_specs(tm, n_rows):
    blocks_per_tile = tm // HALO
    last = n_rows // HALO - 1
    prev = pl.BlockSpec((HALO, D_MODEL), lambda i, s: (jnp.maximum(i * blocks_per_tile - 1, 0), 0))
    nxt = pl.BlockSpec((HALO, D_MODEL), lambda i, s: (jnp.minimum((i + 1) * blocks_per_tile, last), 0))
    return prev, nxt


_MIXER_IN_SECTIONS = (("a", 0), ("v", 2), ("u", 1), ("x_c", 3), ("C", 5), ("B", 4))


def _mixer_in_kernel(xm_ref, xp_ref, xn_ref, g1_ref, w_ref, wpool_ref, pscale_ref, sgug_ref,
                     sguw_ref, sgub_ref, convw_ref, y_ref, h_ref, h_scr, cx_scr, vn_scr, z_scr,
                     *, tm, tiles_per_seq, seq_len):
    i = pl.program_id(0)
    ti = i % tiles_per_seq

    def _pool(win_ref):
        _norm_with_halo(h_scr, xm_ref[...], xp_ref[...], xn_ref[...], g1_ref[...],
                        ti > 0, ti < tiles_per_seq - 1, tm)
        h_ref[...] = h_scr[HALO:HALO + tm, :]
        a = _dot(h_scr[...], win_ref[...])
        pos = ti * tm + lax.broadcasted_iota(jnp.int32, (tm, POOL_GROUP), 0)
        for gi, w in enumerate(POOL_WINDOWS):
            cols = slice(gi * POOL_GROUP, (gi + 1) * POOL_GROUP)
            xg = a[:, cols]
            q, m = xg, 1
            while m < w // 2:
                q = q + _shift_rows(q, m)
                m *= 2
            win_sum = (q + _shift_rows(q, -(w // 2)))[HALO:HALO + tm]
            lo = jnp.maximum(pos - w // 2, 0)
            hi = jnp.minimum(pos + (w - w // 2 - 1), seq_len - 1)
            cnt = (hi - lo + 1).astype(F32)
            pooled = win_sum / cnt - xg[HALO:HALO + tm]
            y = _dot(pooled.astype(BF16), wpool_ref[gi]) * pscale_ref[:, cols]
            y_ref[:, cols] = y.astype(BF16)

    def _sgu_v(win_ref):
        gv = jax.nn.gelu(_dot(h_scr[HALO:HALO + tm, :], win_ref[...]))
        vn_scr[...] = _rmsnorm(gv, sgug_ref[...]).astype(BF16)
        n_chunk = tm // SGU_CHUNK
        for g in range(N_SGU_GROUPS):
            cols = slice(g * SGU_GROUP, (g + 1) * SGU_GROUP)
            rhs = jnp.concatenate(
                [vn_scr[c * SGU_CHUNK:(c + 1) * SGU_CHUNK, cols] for c in range(n_chunk)], axis=1)
            o = _dot(sguw_ref[g], rhs)
            for c in range(n_chunk):
                z_scr[c * SGU_CHUNK:(c + 1) * SGU_CHUNK, cols] = (
                    o[:, c * SGU_GROUP:(c + 1) * SGU_GROUP] + sgub_ref[g])

    def _sgu_u(win_ref):
        u = jax.nn.gelu(_dot(h_scr[HALO:HALO + tm, :], win_ref[...]))
        y_ref[:, D_BRANCH:2 * D_BRANCH] = (u * z_scr[...]).astype(BF16)

    def _conv_x(win_ref):
        cx_scr[...] = _dot(h_scr[...], win_ref[...])

    def _conv_c(win_ref):
        cx_scr[...] = cx_scr[...] * _dot(h_scr[...], win_ref[...])

    def _conv_b(win_ref):
        cx = cx_scr[...]
        conv = (convw_ref[0:1, :] * _shift_rows(cx, -1) + convw_ref[1:2, :] * cx
                + convw_ref[2:3, :] * _shift_rows(cx, 1))[HALO:HALO + tm]
        bg = _dot(h_scr[HALO:HALO + tm, :], win_ref[...])
        y_ref[:, 2 * D_BRANCH:] = (bg * conv).astype(BF16)

    sections = {"a": _pool, "v": _sgu_v, "u": _sgu_u, "x_c": _conv_x, "C": _conv_c, "B": _conv_b}
    for name, blk in _MIXER_IN_SECTIONS:
        sections[name](w_ref.at[:, blk * D_BRANCH:(blk + 1) * D_BRANCH])


def _mixer_in(x2d, layer, p, *, tm, seq_len):
    n_rows = x2d.shape[0]
    prev_spec, next_spec = _halo_specs(tm, n_rows)
    const = lambda *shape: pl.BlockSpec((None,) + shape, lambda i, s: (layer,) + (0,) * len(shape))
    kernel = functools.partial(_mixer_in_kernel, tm=tm, tiles_per_seq=seq_len // tm, seq_len=seq_len)
    return pl.pallas_call(
        kernel,
        grid=(n_rows // tm, 1),
        in_specs=[
            pl.BlockSpec((tm, D_MODEL), lambda i, s: (i, 0)),
            prev_spec, next_spec,
            const(1, D_MODEL),
            pl.BlockSpec((None, D_MODEL, OFF_GATE), lambda i, s: (layer, 0, 0), pipeline_mode=pl.Buffered(1)),
            const(len(POOL_WINDOWS), POOL_GROUP, POOL_GROUP),
            const(1, D_BRANCH),
            const(1, D_BRANCH),
            const(N_SGU_GROUPS, SGU_CHUNK, SGU_CHUNK),
            const(N_SGU_GROUPS, SGU_CHUNK, SGU_GROUP),
            const(3, D_BRANCH),
        ],
        out_specs=[pl.BlockSpec((tm, 3 * D_BRANCH), lambda i, s: (i, 0)),
                   pl.BlockSpec((tm, D_MODEL), lambda i, s: (i, 0))],
        out_shape=[jax.ShapeDtypeStruct((n_rows, 3 * D_BRANCH), BF16),
                   jax.ShapeDtypeStruct((n_rows, D_MODEL), BF16)],
        scratch_shapes=[
            pltpu.VMEM((tm + 2 * HALO, D_MODEL), BF16),
            pltpu.VMEM((tm + 2 * HALO, D_BRANCH), F32),
            pltpu.VMEM((tm, D_BRANCH), BF16),
            pltpu.VMEM((tm, D_BRANCH), F32),
        ],
        compiler_params=pltpu.CompilerParams(
            dimension_semantics=("parallel", "arbitrary"), vmem_limit_bytes=V7X_VMEM_LIMIT_BYTES),
        name="mixer_in",
    )(x2d, x2d, x2d, p["norm1_g"], p["w_in"], p["w_pool"], p["pool_scale"], p["sgu_norm_g"],
      p["sgu_w"], p["sgu_b"], p["conv_w"])


def _mixer_out_kernel(h_ref, y_ref, wga_ref, wgb_ref, wgc_ref, wa_ref, wb_ref, wc_ref, wo_ref, out_ref, m_scr,
                      *, bn_acc):
    s = pl.program_id(1)

    def merged_block(rows):
        h = h_ref[rows, :]
        merged = (jax.nn.sigmoid(_dot(h, wga_ref[...])) * _dot(y_ref[rows, 0:D_BRANCH], wa_ref[...])
                  + jax.nn.sigmoid(_dot(h, wgb_ref[...])) * _dot(y_ref[rows, D_BRANCH:2 * D_BRANCH], wb_ref[...])
                  + jax.nn.sigmoid(_dot(h, wgc_ref[...])) * _dot(y_ref[rows, 2 * D_BRANCH:], wc_ref[...]))
        return merged.astype(BF16)

    row_tiles = [slice(r0, r0 + MXU_ACC_ROWS) for r0 in range(0, h_ref.shape[0], MXU_ACC_ROWS)]

    group = m_scr.shape[0] + 1

    def park(slot):
        for rows in row_tiles:
            m_scr[slot, rows, :] = merged_block(rows)

    def accumulate(first):
        for rows in row_tiles:
            blocks = jnp.concatenate([m_scr[k, rows, :] for k in range(group - 1)] + [merged_block(rows)], axis=1)
            for c in range(D_MODEL // bn_acc):
                cols = slice(c * bn_acc, (c + 1) * bn_acc)
                upd = _dot(blocks, wo_ref[:, cols])
                if first:
                    out_ref[rows, cols] = upd
                else:
                    out_ref[rows, cols] += upd

    for slot in range(group - 1):
        pl.when(s % group == slot)(functools.partial(park, slot))
    pl.when(s == group - 1)(functools.partial(accumulate, True))
    pl.when((s % group == group - 1) & (s >= group))(functools.partial(accumulate, False))


def _mixer_out(h, y, layer, p, *, tm, bn, group):
    n_rows = h.shape[0]
    n_steps = D_MODEL // bn
    assert n_steps % group == 0

    def gate_spec(branch):
        first = (OFF_GATE + branch * D_MODEL) // bn
        return pl.BlockSpec((None, D_MODEL, bn), lambda i, s: (layer, 0, first + s))

    branch_spec = pl.BlockSpec((None, D_BRANCH, bn), lambda i, s: (layer, 0, s))
    kernel = functools.partial(_mixer_out_kernel, bn_acc=512)
    return pl.pallas_call(
        kernel,
        grid=(n_rows // tm, n_steps),
        in_specs=[
            pl.BlockSpec((tm, D_MODEL), lambda i, s: (i, 0)),
            pl.BlockSpec((tm, 3 * D_BRANCH), lambda i, s: (i, 0)),
            gate_spec(0), gate_spec(1), gate_spec(2),
            branch_spec, branch_spec, branch_spec,
            pl.BlockSpec((None, group * bn, D_MODEL), lambda i, s: (layer, s // group, 0)),
        ],
        out_specs=pl.BlockSpec((tm, D_MODEL), lambda i, s: (i, 0)),
        out_shape=jax.ShapeDtypeStruct((n_rows, D_MODEL), F32),
        scratch_shapes=[pltpu.VMEM((group - 1, tm, bn), BF16)],
        compiler_params=pltpu.CompilerParams(
            dimension_semantics=("parallel", "arbitrary"), vmem_limit_bytes=V7X_VMEM_LIMIT_BYTES),
        name="mixer_out",
    )(h, y, p["w_in"], p["w_in"], p["w_in"], p["w_pool_out"], p["w_sgu_out"], p["w_conv_out"], p["w_o"])


def _ffn_kernel(xm_ref, dm_ref, xe_ref, de_ref, g2_ref, wg0_ref, wv0_ref, wg1_ref, wv1_ref, cw_ref, wdp_ref,
                wdl_ref, gf_ref, out_ref, h_scr, he_scr, eg_scr, ev_scr,
                *, tm, n_tiles, tiles_per_seq, n_blocks, final_norm):
    i = pl.program_id(0)
    s = pl.program_id(1)
    ti = i % tiles_per_seq
    n_pairs = n_blocks // 2

    @pl.when(s == 0)
    def _init():
        x_mid = xm_ref[...] + dm_ref[...]
        h_scr[...] = _rmsnorm(x_mid, g2_ref[...]).astype(BF16)
        out_ref[...] = x_mid

    @pl.when((s == 0) & (i == 0))
    def _edge_norm():
        he_scr[...] = _rmsnorm(xe_ref[...] + de_ref[...], g2_ref[...]).astype(BF16)

    row = lax.broadcasted_iota(jnp.int32, (tm, 1), 0)

    def conv3(up, edge_scr, blk, taps):
        cw = cw_ref.at[taps]
        before = jnp.where(ti > 0, edge_scr[blk, pl.ds(i, 1), :], 0.0)
        after = jnp.where(ti < tiles_per_seq - 1, edge_scr[blk, pl.ds(n_tiles + i, 1), :], 0.0)
        prev = jnp.where(row == 0, before, _shift_rows(up, -1))
        nxt = jnp.where(row == tm - 1, after, _shift_rows(up, 1))
        return cw[0:1, :] * prev + cw[1:2, :] * up + cw[2:3, :] * nxt

    def act_block(blk, wg_ref, wv_ref):
        @pl.when(i == 0)
        def _edge_up():
            eg_scr[blk] = _dot(he_scr[...], wg_ref[...])
            ev_scr[blk] = _dot(he_scr[...], wv_ref[...])

        h = h_scr[...]
        gate = conv3(_dot(h, wg_ref[...]), eg_scr, blk, blk)
        val = conv3(_dot(h, wv_ref[...]), ev_scr, blk, n_blocks + blk)
        return (jax.nn.silu(gate) * val).astype(BF16)

    @pl.when(s < n_pairs)
    def _pair():
        act = jnp.concatenate([act_block(2 * s, wg0_ref, wv0_ref), act_block(2 * s + 1, wg1_ref, wv1_ref)],
                              axis=1)
        out_ref[...] += _dot(act, wdp_ref[...])

    @pl.when(s == n_pairs)
    def _last():
        out = out_ref[...] + _dot(act_block(n_blocks - 1, wg0_ref, wv0_ref), wdl_ref[...])
        if final_norm:
            out = _rmsnorm(out, gf_ref[...])
        out_ref[...] = out


def _ffn(x2d, delta, layer, p, final_g, *, tm, bn, seq_len, final_norm):
    n_rows = x2d.shape[0]
    n_tiles = n_rows // tm
    n_blocks = D_FF // bn
    assert n_blocks % 2 == 1
    n_pairs = n_blocks // 2
    starts = [t * tm for t in range(n_tiles)]
    edge_rows = jnp.asarray([max(r - 1, 0) for r in starts] + [min(r + tm, n_rows - 1) for r in starts],
                            dtype=jnp.int32)
    x_edge = jnp.take(x2d, edge_rows, axis=0)
    d_edge = jnp.take(delta, edge_rows, axis=0)
    n_edge = 2 * n_tiles
    cw = p["ffn_conv_w"]
    conv_w_blocks = jnp.transpose(cw.reshape(cw.shape[0], 3, 2 * n_blocks, bn), (0, 2, 1, 3))

    main_spec = pl.BlockSpec((tm, D_MODEL), lambda i, s: (i, 0))
    edge_spec = pl.BlockSpec((n_edge, D_MODEL), lambda i, s: (0, 0))
    first_blk = lambda s: jnp.minimum(2 * s, n_blocks - 1)
    second_blk = lambda s: jnp.minimum(2 * s + 1, n_blocks - 1)

    def up_spec(half, blk):
        return pl.BlockSpec((None, D_MODEL, bn), lambda i, s: (layer, 0, half * n_blocks + blk(s)))

    kernel = functools.partial(_ffn_kernel, tm=tm, n_tiles=n_tiles, tiles_per_seq=seq_len // tm,
                               n_blocks=n_blocks, final_norm=final_norm)
    return pl.pallas_call(
        kernel,
        grid=(n_tiles, n_pairs + 1),
        in_specs=[
            main_spec, main_spec,
            edge_spec, edge_spec,
            pl.BlockSpec((None, 1, D_MODEL), lambda i, s: (layer, 0, 0)),
            up_spec(0, first_blk), up_spec(1, first_blk),
            up_spec(0, second_blk), up_spec(1, second_blk),
            pl.BlockSpec((None, 2 * n_blocks, 3, bn), lambda i, s: (layer, 0, 0, 0)),
            pl.BlockSpec((None, 2 * bn, D_MODEL),
                         lambda i, s: (layer, jnp.minimum(s, n_pairs - 1), 0)),
            pl.BlockSpec((None, bn, D_MODEL), lambda i, s: (layer, n_blocks - 1, 0),
                         pipeline_mode=pl.Buffered(1)),
            pl.BlockSpec((1, D_MODEL), lambda i, s: (0, 0)),
        ],
        out_specs=main_spec,
        out_shape=jax.ShapeDtypeStruct((n_rows, D_MODEL), F32),
        scratch_shapes=[pltpu.VMEM((tm, D_MODEL), BF16),
                        pltpu.VMEM((n_edge, D_MODEL), BF16),
                        pltpu.VMEM((n_blocks, n_edge, bn), F32),
                        pltpu.VMEM((n_blocks, n_edge, bn), F32)],
        compiler_params=pltpu.CompilerParams(
            dimension_semantics=("arbitrary", "arbitrary"), vmem_limit_bytes=V7X_VMEM_LIMIT_FFN_BYTES),
        name="ffn",
    )(x2d, delta, x_edge, d_edge, p["norm2_g"], p["w_up"], p["w_up"], p["w_up"], p["w_up"], conv_w_blocks,
      p["w_down"], p["w_down"], final_g)


def kernel(x, norm1_g, w_in, w_pool, pool_scale, sgu_norm_g, sgu_w, sgu_b, conv_w, w_pool_out, w_sgu_out,
           w_conv_out, w_o, norm2_g, w_up, ffn_conv_w, w_down, final_g):
    bsz, seq_len, d_model = x.shape
    depth = w_in.shape[0]
    assert d_model == D_MODEL and w_in.shape[2] == OFF_GATE + 3 * D_MODEL and w_up.shape[2] == 2 * D_FF
    tm_in, tm_out, tm_ffn = 512, 1024, 512
    assert all(seq_len % t == 0 and t % SGU_CHUNK == 0 for t in (tm_in, tm_out, tm_ffn))

    p = {
        "norm1_g": norm1_g[:, None, :], "norm2_g": norm2_g[:, None, :],
        "pool_scale": pool_scale[:, None, :], "sgu_norm_g": sgu_norm_g[:, None, :],
        "w_in": w_in.astype(BF16), "w_pool": w_pool.astype(BF16), "sgu_w": sgu_w.astype(BF16),
        "sgu_b": jnp.broadcast_to(jnp.transpose(sgu_b, (0, 2, 1))[..., None],
                                  (depth, N_SGU_GROUPS, SGU_CHUNK, SGU_GROUP)),
        "conv_w": conv_w, "ffn_conv_w": ffn_conv_w,
        "w_pool_out": w_pool_out.astype(BF16), "w_sgu_out": w_sgu_out.astype(BF16),
        "w_conv_out": w_conv_out.astype(BF16), "w_o": w_o.astype(BF16),
        "w_up": w_up.astype(BF16), "w_down": w_down.astype(BF16),
    }
    x2d = x.reshape(bsz * seq_len, d_model)
    for layer in range(depth):
        y, h = _mixer_in(x2d, layer, p, tm=tm_in, seq_len=seq_len)
        delta = _mixer_out(h, y, layer, p, tm=tm_out, bn=256, group=4)
        x2d = _ffn(x2d, delta, layer, p, final_g[None, :], tm=tm_ffn, bn=512, seq_len=seq_len,
                   final_norm=(layer == depth - 1))
    return x2d.reshape(bsz, seq_len, d_model)
```
